```python
import jax, jax.numpy as jnp
from jax import lax
import numpy as np

D_MODEL = 1024
BATCH = 32
SEQ = 2048
DEPTH = 1
DEC_BATCH = 128
DEC_SEQ = 1
PAST_LEN = 8192
PAGE_SIZE = 128

N_META = 16
FOX_HEADS = 8
FOX_DH = 64
FOX_W = FOX_HEADS * FOX_DH
Q_BLOCK = 128
FORGET_BIAS = 4.0
GLA_HEADS = 4
GLA_DK = 128
GLA_DV = 256
GLA_KW = GLA_HEADS * GLA_DK
GLA_VW = GLA_HEADS * GLA_DV
GLA_RANK = 16
GLA_TAU = 16.0
GLA_CHUNK = 64
D_FF = 2816
CONV_W = 3
EPS = 1e-6

OFF_FQ = 0
OFF_FK = OFF_FQ + FOX_W
OFF_FV = OFF_FK + FOX_W
OFF_FF = OFF_FV + FOX_W
OFF_GQ = OFF_FF + FOX_HEADS
OFF_GK = OFF_GQ + GLA_KW
OFF_GV = OFF_GK + GLA_KW
OFF_GR = OFF_GV + GLA_VW
OFF_GA = OFF_GR + GLA_VW
OFF_MA = OFF_GA + GLA_RANK
OFF_MB = OFF_MA + D_MODEL
IN_COLS = OFF_MB + D_MODEL

kernel_name = 'fox_gla_gated_merge_convffn_step'


def rmsnorm(x, g):
    xf = x.astype(jnp.float32)
    y = xf * lax.rsqrt(jnp.mean(xf * xf, axis=-1, keepdims=True) + EPS)
    return (y * g.astype(jnp.float32)).astype(x.dtype)


def in_project(h, w_in, b_forget, w_gla_gate_up, b_gla_gate):
    p = h @ w_in
    lead = p.shape[:-1]
    fq = p[..., OFF_FQ:OFF_FK].reshape(*lead, FOX_HEADS, FOX_DH)
    fk = p[..., OFF_FK:OFF_FV].reshape(*lead, FOX_HEADS, FOX_DH)
    fv = p[..., OFF_FV:OFF_FF].reshape(*lead, FOX_HEADS, FOX_DH)
    logf = jax.nn.log_sigmoid((p[..., OFF_FF:OFF_GQ] + b_forget).astype(jnp.float32))
    gq = p[..., OFF_GQ:OFF_GK].reshape(*lead, GLA_HEADS, GLA_DK)
    gk = p[..., OFF_GK:OFF_GV].reshape(*lead, GLA_HEADS, GLA_DK)
    gv = p[..., OFF_GV:OFF_GR].reshape(*lead, GLA_HEADS, GLA_DV)
    gr = p[..., OFF_GR:OFF_GA]
    glog = (jax.nn.log_sigmoid((p[..., OFF_GA:OFF_MA] @ w_gla_gate_up + b_gla_gate).astype(jnp.float32))
            / GLA_TAU).reshape(*lead, GLA_HEADS, GLA_DK)
    ma = jax.nn.sigmoid(p[..., OFF_MA:OFF_MB])
    mb = jax.nn.sigmoid(p[..., OFF_MB:IN_COLS])
    return fq, fk, fv, logf, gq, gk, gv, gr, glog, ma, mb


def fox_prompt(q, k, v, logf):
    B, T = q.shape[0], q.shape[1]
    n_blk = -(-T // Q_BLOCK)
    Tp = n_blk * Q_BLOCK
    pad = Tp - T
    c = jnp.cumsum(logf, axis=1)
    pad4 = ((0, 0), (0, pad), (0, 0), (0, 0))
    q = jnp.pad(q, pad4)
    k = jnp.pad(k, pad4)
    v = jnp.pad(v, pad4)
    cT = jnp.pad(c, ((0, 0), (0, pad), (0, 0))).transpose(0, 2, 1)
    kpos = jnp.arange(Tp)
    scale = FOX_DH ** -0.5

    def block(i):
        start = i * Q_BLOCK
        qb = lax.dynamic_slice_in_dim(q, start, Q_BLOCK, axis=1)
        cq = lax.dynamic_slice_in_dim(cT, start, Q_BLOCK, axis=2)
        s = jnp.einsum('bqhd,bkhd->bhqk', qb, k).astype(jnp.float32) * scale
        s = s + cq[..., :, None] - cT[..., None, :]
        qpos = start + jnp.arange(Q_BLOCK)
        s = jnp.where(kpos[None, :] <= qpos[:, None], s, -jnp.inf)
        pr = jax.nn.softmax(s, axis=-1).astype(v.dtype)
        return jnp.einsum('bhqk,bkhd->bqhd', pr, v)

    o = lax.map(block, jnp.arange(n_blk))
    return o.transpose(1, 0, 2, 3, 4).reshape(B, Tp, FOX_W)[:, :T]


def fox_sample(q, k, v, logf, kp, vp, lp):
    DB, DS = q.shape[0], q.shape[1]
    kp = kp.reshape(DB, -1, FOX_HEADS, FOX_DH)
    vp = vp.reshape(DB, -1, FOX_HEADS, FOX_DH)
    lp = lp.reshape(DB, -1, FOX_HEADS).astype(jnp.float32)
    P = kp.shape[1]
    suf = lax.cumsum(lp, axis=1, reverse=True)
    suf = jnp.concatenate([suf[:, 1:], jnp.zeros_like(suf[:, :1])], axis=1).transpose(0, 2, 1)
    cn = jnp.cumsum(logf, axis=1).transpose(0, 2, 1)
    scale = FOX_DH ** -0.5
    s_past = jnp.einsum('bqhd,bkhd->bhqk', q, kp).astype(jnp.float32) * scale + cn[..., :, None] + suf[..., None, :]
    s_new = jnp.einsum('bqhd,bkhd->bhqk', q, k).astype(jnp.float32) * scale + cn[..., :, None] - cn[..., None, :]
    s_new = jnp.where(jnp.tril(jnp.ones((DS, DS), bool)), s_new, -jnp.inf)
    pr = jax.nn.softmax(jnp.concatenate([s_past, s_new], axis=-1), axis=-1).astype(v.dtype)
    o = jnp.einsum('bhqk,bkhd->bqhd', pr[..., :P], vp) + jnp.einsum('bhqk,bkhd->bqhd', pr[..., P:], v)
    return o.reshape(DB, DS, FOX_W)


def gla_prompt(q, k, v, g):
    B, T = q.shape[0], q.shape[1]
    lead = (-N_META) % GLA_CHUNK
    tail = (-(lead + T)) % GLA_CHUNK
    n_chk = (lead + T + tail) // GLA_CHUNK

    def chunks(a):
        a = jnp.pad(a, ((0, 0), (lead, tail), (0, 0), (0, 0)))
        return a.reshape(B, n_chk, GLA_CHUNK, GLA_HEADS, a.shape[-1]).transpose(1, 0, 3, 2, 4)

    q = chunks(q) * GLA_DK ** -0.5
    k = chunks(k)
    v = chunks(v)
    g = chunks(g)
    b = jnp.cumsum(g, axis=3)
    b_last = b[:, :, :, -1:, :]
    qg = q * jnp.exp(b)
    a = jnp.einsum('nbhtd,nbhsd->nbhts', qg, k * jnp.exp(-b))
    a = jnp.where(jnp.tril(jnp.ones((GLA_CHUNK, GLA_CHUNK), bool)), a, 0.0)
    o_intra = jnp.einsum('nbhts,nbhse->nbhte', a, v)
    kd = k * jnp.exp(b_last - b)

    def step(S, xs):
        qg_n, kd_n, v_n, bl_n = xs
        o = jnp.einsum('bhtd,bhde->bhte', qg_n, S)
        S = S * jnp.exp(bl_n[:, :, 0, :, None]) + jnp.einsum('bhsd,bhse->bhde', kd_n, v_n)
        return S, o

    S0 = jnp.zeros((B, GLA_HEADS, GLA_DK, GLA_DV), jnp.float32)
    S, o_inter = lax.scan(step, S0, (qg, kd, v, b_last))
    o = (o_intra + o_inter).transpose(1, 0, 3, 2, 4).reshape(B, n_chk * GLA_CHUNK, GLA_HEADS, GLA_DV)
    return o[:, lead:lead + T], S


def gla_sample(q, k, v, g, S):
    q = q * GLA_DK ** -0.5

    def step(S, xs):
        q_t, k_t, v_t, g_t = xs
        S = S * jnp.exp(g_t)[..., None] + k_t[..., :, None] * v_t[..., None, :]
        return S, jnp.einsum('bhd,bhde->bhe', q_t, S)

    S, o = lax.scan(step, S.astype(jnp.float32), (q.swapaxes(0, 1), k.swapaxes(0, 1), v.swapaxes(0, 1), g.swapaxes(0, 1)))
    return o.swapaxes(0, 1), S


def mix_merge(fox_o, gla_o, gr, ma, mb, gla_norm_g, w_fox_up, w_gla_up, w_out):
    lead = gr.shape[:-1]
    go = rmsnorm(gla_o.astype(gr.dtype), gla_norm_g).reshape(*lead, GLA_VW) * jax.nn.silu(gr)
    y = ma * (fox_o @ w_fox_up) + mb * (go @ w_gla_up)
    return y @ w_out


def conv_ffn(h, conv_buf, w_up, conv_w, conv_b, w_down):
    T = h.shape[1]
    u = h @ w_up
    up = jnp.concatenate([conv_buf.astype(u.dtype), u], axis=1)
    c = conv_b + up[:, 0:T] * conv_w[0]
    for i in range(1, CONV_W):
        c = c + up[:, i:i + T] * conv_w[i]
    val, gate = c[..., :D_FF], c[..., D_FF:]
    y = (jax.nn.gelu(gate) * val) @ w_down
    return y, up[:, T:]


def setup_inputs(seed: int = 0) -> dict:
    key = jax.random.key(seed)
    ks = iter(jax.random.split(key, 32))

    def nrm(shape, scale=1.0):
        return jax.random.normal(next(ks), shape, jnp.float32) * scale

    n_pages = PAST_LEN // PAGE_SIZE
    n_used = DEC_BATCH * n_pages
    n_pool = n_used + n_used // 4
    page_table = jax.random.permutation(next(ks), n_pool)[:n_used].reshape(DEC_BATCH, n_pages).astype(jnp.int32)
    L = DEPTH
    return {
        'x_prompt': nrm((BATCH, SEQ, D_MODEL)),
        'x_sample': nrm((DEC_BATCH, DEC_SEQ, D_MODEL)),
        'cache_k': nrm((L, n_pool, PAGE_SIZE, FOX_HEADS, FOX_DH)),
        'cache_v': nrm((L, n_pool, PAGE_SIZE, FOX_HEADS, FOX_DH)),
        'cache_logf': jax.nn.log_sigmoid(FORGET_BIAS + nrm((L, n_pool, PAGE_SIZE, FOX_HEADS), 0.5)),
        'state_gla': nrm((L, DEC_BATCH, GLA_HEADS, GLA_DK, GLA_DV)),
        'state_conv': nrm((L, DEC_BATCH, CONV_W - 1, 2 * D_FF)),
        'page_table': page_table,
        'meta_tokens': nrm((N_META, D_MODEL)),
        'g_pre_mix': 1.0 + nrm((L, D_MODEL), 0.1),
        'w_in': nrm((L, D_MODEL, IN_COLS), D_MODEL ** -0.5),
        'b_forget': FORGET_BIAS + nrm((L, FOX_HEADS), 0.5),
        'w_gla_gate_up': nrm((L, GLA_RANK, GLA_KW), GLA_RANK ** -0.5),
        'b_gla_gate': nrm((L, GLA_KW), 0.1),
        'gla_norm_g': 1.0 + nrm((L, GLA_DV), 0.1),
        'w_fox_up': nrm((L, FOX_W, D_MODEL), FOX_W ** -0.5),
        'w_gla_up': nrm((L, GLA_VW, D_MODEL), GLA_VW ** -0.5),
        'w_out': nrm((L, D_MODEL, D_MODEL), D_MODEL ** -0.5),
        'g_post_mix': 1.0 + nrm((L, D_MODEL), 0.1),
        'g_pre_ffn': 1.0 + nrm((L, D_MODEL), 0.1),
        'w_up': nrm((L, D_MODEL, 2 * D_FF), D_MODEL ** -0.5),
        'conv_w': nrm((L, CONV_W, 2 * D_FF), CONV_W ** -0.5),
        'conv_b': nrm((L, 2 * D_FF), 0.02),
        'w_down': nrm((L, D_FF, D_MODEL), D_FF ** -0.5),
        'g_post_ffn': 1.0 + nrm((L, D_MODEL), 0.1),
    }


def reference(x_prompt, x_sample, cache_k, cache_v, cache_logf, state_gla, state_conv, page_table,
              meta_tokens, g_pre_mix, w_in, b_forget, w_gla_gate_up, b_gla_gate, gla_norm_g,
              w_fox_up, w_gla_up, w_out, g_post_mix, g_pre_ffn, w_up, conv_w, conv_b, w_down, g_post_ffn):
    B = x_prompt.shape[0]
    meta = jnp.broadcast_to(meta_tokens.astype(x_prompt.dtype), (B, N_META, D_MODEL))
    xp = jnp.concatenate([meta, x_prompt], axis=1)
    xs = x_sample
    kp_l, vp_l, lp_l, sp_l, cp_l = [], [], [], [], []
    ks_l, vs_l, ls_l, ss_l, cs_l = [], [], [], [], []
    for l in range(DEPTH):
        hp = rmsnorm(xp, g_pre_mix[l])
        fq, fk, fv, lf, gq, gk, gv, gr, gg, ma, mb = in_project(hp, w_in[l], b_forget[l], w_gla_gate_up[l], b_gla_gate[l])
        fo = fox_prompt(fq, fk, fv, lf)
        go, sp = gla_prompt(gq, gk, gv, gg)
        m = mix_merge(fo, go, gr, ma, mb, gla_norm_g[l], w_fox_up[l], w_gla_up[l], w_out[l])
        xp = xp + rmsnorm(m, g_post_mix[l])
        zero_buf = jnp.zeros((B, CONV_W - 1, 2 * D_FF), xp.dtype)
        f, cp = conv_ffn(rmsnorm(xp, g_pre_ffn[l]), zero_buf, w_up[l], conv_w[l], conv_b[l], w_down[l])
        xp = xp + rmsnorm(f, g_post_ffn[l])
        kp_l.append(fk); vp_l.append(fv); lp_l.append(lf); sp_l.append(sp); cp_l.append(cp)

        hs = rmsnorm(xs, g_pre_mix[l])
        fq, fk, fv, lf, gq, gk, gv, gr, gg, ma, mb = in_project(hs, w_in[l], b_forget[l], w_gla_gate_up[l], b_gla_gate[l])
        fo = fox_sample(fq, fk, fv, lf, cache_k[l, page_table], cache_v[l, page_table], cache_logf[l, page_table])
        go, ss = gla_sample(gq, gk, gv, gg, state_gla[l])
        m = mix_merge(fo, go, gr, ma, mb, gla_norm_g[l], w_fox_up[l], w_gla_up[l], w_out[l])
        xs = xs + rmsnorm(m, g_post_mix[l])
        f, cs = conv_ffn(rmsnorm(xs, g_pre_ffn[l]), state_conv[l], w_up[l], conv_w[l], conv_b[l], w_down[l])
        xs = xs + rmsnorm(f, g_post_ffn[l])
        ks_l.append(fk); vs_l.append(fv); ls_l.append(lf); ss_l.append(ss); cs_l.append(cs)

    y_prompt = xp[:, N_META:]
    return (y_prompt, xs,
            jnp.stack(kp_l), jnp.stack(vp_l), jnp.stack(lp_l), jnp.stack(sp_l), jnp.stack(cp_l),
            jnp.stack(ks_l), jnp.stack(vs_l), jnp.stack(ls_l), jnp.stack(ss_l), jnp.stack(cs_l))
```

```python
import functools

import jax
import jax.numpy as jnp
from jax import lax
from jax.experimental import pallas as pl
from jax.experimental.pallas import tpu as pltpu

D_MODEL = 1024
N_META = 16
FOX_HEADS = 8
FOX_DH = 64
FOX_W = FOX_HEADS * FOX_DH
GLA_HEADS = 4
GLA_DK = 128
GLA_DV = 256
GLA_KW = GLA_HEADS * GLA_DK
GLA_VW = GLA_HEADS * GLA_DV
GLA_RANK = 16
GLA_TAU = 16.0
GLA_CHUNK = 64
D_FF = 2816
CONV_W = 3
EPS = 1e-6
PAGE_SIZE = 128

LANES = 128
ATT_BLOCK = 256
GLA_BLOCK = 256
FFN_COLS = 256
PAGES_PER_STEP = 16
NEG = -1e30
VMEM_LIMIT = 56 * 1024 * 1024

F32 = jnp.float32
BF16 = jnp.bfloat16

_NT = (((1,), (1,)), ((), ()))


def _dot(a, b):
    return jnp.dot(a, b, preferred_element_type=F32)


def _dot_nt(a, b):
    return lax.dot_general(a, b, _NT, preferred_element_type=F32)


def _rms(x, g):
    return x * lax.rsqrt(jnp.mean(x * x, axis=-1, keepdims=True) + EPS) * g


def _log_sigmoid(x):
    return jnp.minimum(x, 0.0) - jnp.log1p(jnp.exp(-jnp.abs(x)))


def _sigmoid(x):
    return 1.0 / (1.0 + jnp.exp(-x))


def _split3(x):
    x1 = x.astype(BF16)
    r1 = x - x1.astype(F32)
    x2 = r1.astype(BF16)
    x3 = (r1 - x2.astype(F32)).astype(BF16)
    return x1, x2, x3


def _split3_cat(x):
    return jnp.concatenate(_split3(x), axis=1)


def _sum3(y, w):
    return y[:, :w] + y[:, w:2 * w] + y[:, 2 * w:]


def _exact_left(mat01, x):
    return _sum3(_dot(mat01, _split3_cat(x)), x.shape[1])


def _const_spec(shape):
    return pl.BlockSpec(shape, lambda *_: (0,) * len(shape), pipeline_mode=pl.Buffered(1))


def _params(*sem):
    return pltpu.CompilerParams(dimension_semantics=sem, vmem_limit_bytes=VMEM_LIMIT)


def _fox_proj_kernel(x_ref, g_ref, wq_ref, wk_ref, wv_ref, wf_ref, bf_ref, *rest, tiles_per_seq):
    if tiles_per_seq:
        tri_ref, place_ref, ones_ref, fq_ref, k_ref, v_ref, lf_ref, cq_ref, ck_ref, carry_ref = rest
    else:
        fq_ref, k_ref, v_ref, lf_ref = rest
    h = _rms(x_ref[...], g_ref[...]).astype(BF16)
    fq_ref[...] = _dot(h, wq_ref[...]).astype(fq_ref.dtype)
    k_ref[...] = _dot(h, wk_ref[...])
    v_ref[...] = _dot(h, wv_ref[...])
    lane = lax.broadcasted_iota(jnp.int32, (1, LANES), 1)
    lf = jnp.where(lane < FOX_HEADS, _log_sigmoid(_dot(h, wf_ref[...]) + bf_ref[...]), 0.0)
    lf_ref[...] = lf[:, :FOX_HEADS]
    if tiles_per_seq:
        @pl.when(pl.program_id(0) % tiles_per_seq == 0)
        def _():
            carry_ref[...] = jnp.zeros_like(carry_ref)

        tm = lf.shape[0]
        c = carry_ref[...] + _exact_left(tri_ref[...], lf)
        carry_ref[...] = c[tm - 1:tm, :]
        aug = _dot(_split3_cat(c), place_ref[...]) + ones_ref[...]
        cq_ref[...] = aug[:, :FOX_W].astype(BF16)
        ck_ref[...] = aug[:, FOX_W:].astype(BF16)


def _placement():
    import numpy as np
    place = np.zeros((3 * LANES, 2 * FOX_W), np.float32)
    ones = np.zeros((1, 2 * FOX_W), np.float32)
    for h in range(FOX_HEADS):
        base = LANES * (h // 2) + (FOX_DH if h % 2 == 0 else 0)
        for t in range(3):
            place[LANES * t + h, base + t] = 1.0
            ones[0, base + 3 + t] = 1.0
            ones[0, FOX_W + base + t] = 1.0
            place[LANES * t + h, FOX_W + base + 3 + t] = -1.0
    return jnp.asarray(place, BF16), jnp.asarray(ones, F32)


def _fox_proj(x, g, wq, wk, wv, wf, bf, *, tm, tiles_per_seq, q_dtype):
    n = x.shape[0]
    row = lambda w: pl.BlockSpec((tm, w), lambda i: (i, 0))
    in_specs = [row(D_MODEL), _const_spec((1, D_MODEL)), _const_spec((D_MODEL, FOX_W)),
                _const_spec((D_MODEL, FOX_W)), _const_spec((D_MODEL, FOX_W)),
                _const_spec((D_MODEL, LANES)), _const_spec((1, LANES))]
    out_shape = [jax.ShapeDtypeStruct((n, FOX_W), q_dtype), jax.ShapeDtypeStruct((n, FOX_W), F32),
                 jax.ShapeDtypeStruct((n, FOX_W), F32), jax.ShapeDtypeStruct((n, FOX_HEADS), F32)]
    out_specs = [row(FOX_W), row(FOX_W), row(FOX_W), row(FOX_HEADS)]
    args = [x, g, wq, wk, wv, wf, bf]
    scratch = []
    if tiles_per_seq:
        place, ones = _placement()
        tri = jnp.tril(jnp.ones((tm, tm), BF16))
        args += [tri, place, ones]
        in_specs += [_const_spec((tm, tm)), _const_spec((3 * LANES, 2 * FOX_W)), _const_spec((1, 2 * FOX_W))]
        out_shape += [jax.ShapeDtypeStruct((n, FOX_W), BF16)] * 2
        out_specs += [row(FOX_W), row(FOX_W)]
        scratch = [pltpu.VMEM((1, LANES), F32)]
    return pl.pallas_call(
        functools.partial(_fox_proj_kernel, tiles_per_seq=tiles_per_seq),
        grid=(n // tm,), in_specs=in_specs, out_specs=out_specs, out_shape=out_shape,
        scratch_shapes=scratch, compiler_params=_params("arbitrary"), name="fox_proj")(*args)


def _gla_proj_kernel(x_ref, g_ref, wq_ref, wk_ref, wv_ref, wr_ref, wa_ref, wu_ref, bu_ref, wma_ref, wmb_ref,
                     gq_ref, gk_ref, gv_ref, gl_ref, sr_ref, ma_ref, mb_ref):
    h = _rms(x_ref[...], g_ref[...]).astype(BF16)
    gq_ref[...] = _dot(h, wq_ref[...]).astype(gq_ref.dtype)
    gk_ref[...] = _dot(h, wk_ref[...]).astype(gk_ref.dtype)
    gv_ref[...] = _dot(h, wv_ref[...]).astype(gv_ref.dtype)
    low = _dot(h, wa_ref[...]).astype(BF16)
    gl_ref[...] = _log_sigmoid(_dot(low, wu_ref[...]) + bu_ref[...]) * (1.0 / GLA_TAU)
    r = _dot(h, wr_ref[...])
    sr_ref[...] = (r * _sigmoid(r)).astype(sr_ref.dtype)
    ma_ref[...] = _sigmoid(_dot(h, wma_ref[...])).astype(ma_ref.dtype)
    mb_ref[...] = _sigmoid(_dot(h, wmb_ref[...])).astype(mb_ref.dtype)


def _gla_proj(x, g, wq, wk, wv, wr, wa, wu, bu, wma, wmb, *, tm, mid_dtype):
    n = x.shape[0]
    row = lambda w: pl.BlockSpec((tm, w), lambda i: (i, 0))
    widths = [GLA_KW, GLA_KW, GLA_VW, GLA_KW, GLA_VW, D_MODEL, D_MODEL]
    dtypes = [mid_dtype, mid_dtype, mid_dtype, F32, mid_dtype, mid_dtype, mid_dtype]
    return pl.pallas_call(
        _gla_proj_kernel, grid=(n // tm,),
        in_specs=[row(D_MODEL), _const_spec((1, D_MODEL)), _const_spec((D_MODEL, GLA_KW)),
                  _const_spec((D_MODEL, GLA_KW)), _const_spec((D_MODEL, GLA_VW)), _const_spec((D_MODEL, GLA_VW)),
                  _const_spec((D_MODEL, LANES)), _const_spec((LANES, GLA_KW)), _const_spec((1, GLA_KW)),
                  _const_spec((D_MODEL, D_MODEL)), _const_spec((D_MODEL, D_MODEL))],
        out_specs=[row(w) for w in widths],
        out_shape=[jax.ShapeDtypeStruct((n, w), dt) for w, dt in zip(widths, dtypes)],
        compiler_params=_params("parallel"), name="gla_proj")(x, g, wq, wk, wv, wr, wa, wu, bu, wma, wmb)


def _fox_prompt_kernel(fq_ref, k_ref, v_ref, cq_ref, ck_ref, o_ref, q_s, k_s, v_s, m_s, l_s, acc_s, o_s, *, seq):
    n_full = (seq - N_META) // ATT_BLOCK
    lane = lax.broadcasted_iota(jnp.int32, (1, LANES), 1)
    lo = lane < FOX_DH
    qp, cq = fq_ref[0], cq_ref[0]
    q_s[0] = jnp.where(lo, qp, cq)
    q_s[1] = jnp.where(lo, cq, qp)
    kp, ck = k_ref[0].astype(BF16), ck_ref[0]
    k_s[0] = jnp.where(lo, kp, ck)
    k_s[1] = jnp.where(lo, ck, kp)
    v_s[...] = v_ref[0].astype(BF16)

    def init(rows):
        m_s[:rows] = jnp.full((rows, LANES), NEG, F32)
        l_s[:rows] = jnp.zeros((rows, LANES), F32)
        acc_s[:rows] = jnp.zeros((rows, LANES), F32)

    def attend(e, q, rows, kstart, mask):
        s = _dot_nt(q, k_s[e, pl.ds(kstart, ATT_BLOCK), :])
        if mask is not None:
            s = jnp.where(mask, s, NEG)
        m_prev = m_s[:rows]
        m_next = jnp.maximum(m_prev, jnp.max(s, axis=1, keepdims=True))
        p = jnp.exp(s - jnp.concatenate([m_next] * (ATT_BLOCK // LANES), axis=1))
        alpha = jnp.exp(m_prev - m_next)
        l_s[:rows] = alpha * l_s[:rows] + jnp.sum(p, axis=1, keepdims=True)
        acc_s[:rows] = alpha * acc_s[:rows] + _dot(p.astype(BF16), v_s[pl.ds(kstart, ATT_BLOCK), :])
        m_s[:rows] = m_next

    r_io = lax.broadcasted_iota(jnp.int32, (ATT_BLOCK, ATT_BLOCK), 0)
    c_io = lax.broadcasted_iota(jnp.int32, (ATT_BLOCK, ATT_BLOCK), 1)
    diag_mask = c_io <= r_io

    def q_block(i, carry):
        r0 = pl.multiple_of(i * ATT_BLOCK, ATT_BLOCK)
        for e in range(2):
            q = q_s[e, pl.ds(r0, ATT_BLOCK), :]
            init(ATT_BLOCK)

            def kv_block(j, c, e=e, q=q):
                attend(e, q, ATT_BLOCK, pl.multiple_of(j * ATT_BLOCK, ATT_BLOCK), None)
                return c

            lax.fori_loop(0, i, kv_block, 0)
            attend(e, q, ATT_BLOCK, r0, diag_mask)
            o_s[e] = acc_s[...] / l_s[...]
        o_ref[0, pl.ds(r0, ATT_BLOCK), :] = jnp.where(lo, o_s[0], o_s[1]).astype(o_ref.dtype)
        return carry

    lax.fori_loop(0, n_full, q_block, 0)

    r0 = n_full * ATT_BLOCK
    tr = lax.broadcasted_iota(jnp.int32, (N_META, ATT_BLOCK), 0)
    tc = lax.broadcasted_iota(jnp.int32, (N_META, ATT_BLOCK), 1)
    first_new = ATT_BLOCK - N_META
    tail_mask = (tc >= first_new) & (tc - first_new <= tr)
    for e in range(2):
        q = q_s[e, r0:r0 + N_META, :]
        init(N_META)

        def kv_tail(j, c, e=e, q=q):
            attend(e, q, N_META, pl.multiple_of(j * ATT_BLOCK, ATT_BLOCK), None)
            return c

        lax.fori_loop(0, n_full, kv_tail, 0)
        attend(e, q, N_META, seq - ATT_BLOCK, tail_mask)
        o_s[e, :N_META] = acc_s[:N_META] / l_s[:N_META]
    o_ref[0, r0:r0 + N_META, :] = jnp.where(lo, o_s[0, :N_META], o_s[1, :N_META]).astype(o_ref.dtype)


def _fox_prompt(fq, k, v, cq, ck):
    b, seq, _ = fq.shape
    assert (seq - N_META) % ATT_BLOCK == 0 and seq >= ATT_BLOCK
    spec = pl.BlockSpec((1, seq, LANES), lambda i, j: (i, 0, j))
    return pl.pallas_call(
        functools.partial(_fox_prompt_kernel, seq=seq), grid=(b, FOX_W // LANES),
        in_specs=[spec] * 5, out_specs=spec, out_shape=jax.ShapeDtypeStruct((b, seq, FOX_W), BF16),
        scratch_shapes=[pltpu.VMEM((2, seq, LANES), BF16), pltpu.VMEM((2, seq, LANES), BF16),
                        pltpu.VMEM((seq, LANES), BF16), pltpu.VMEM((ATT_BLOCK, LANES), F32),
                        pltpu.VMEM((ATT_BLOCK, LANES), F32), pltpu.VMEM((ATT_BLOCK, LANES), F32),
                        pltpu.VMEM((2, ATT_BLOCK, LANES), F32)],
        compiler_params=_params("parallel", "parallel"), name="fox_prompt")(fq, k, v, cq, ck)


def _gla_prompt_kernel(q_ref, k_ref, g_ref, v_ref, gn_ref, tri_ref, ones_ref, o_ref, st_ref, s_s, *, seq):
    n_blocks = (seq - N_META) // GLA_BLOCK
    scale = GLA_DK ** -0.5
    gn = gn_ref[...]
    s_s[...] = jnp.zeros_like(s_s)

    def block(q, k, g, v, rows):
        r_io = lax.broadcasted_iota(jnp.int32, (rows, rows), 0)
        c_io = lax.broadcasted_iota(jnp.int32, (rows, rows), 1)
        intra = (r_io // GLA_CHUNK == c_io // GLA_CHUNK) & (c_io <= r_io)
        col_chunk = lax.broadcasted_iota(jnp.int32, (GLA_DK, rows), 1) // GLA_CHUNK
        tri = tri_ref[:rows, :rows]
        ones = ones_ref[:rows, :rows]
        g3 = _split3_cat(g)
        b = _sum3(_dot(tri, g3), GLA_DK)
        b_last = _sum3(_dot(ones, g3), GLA_DK)
        kf = k.astype(F32)
        qg = (q.astype(F32) * (scale * jnp.exp(b))).astype(BF16)
        kb = (kf * jnp.exp(-b)).astype(BF16)
        kd_t = (kf * jnp.exp(b_last - b)).T
        decay_t = jnp.exp(b_last).T
        a = jnp.where(intra, _dot_nt(qg, kb), 0.0).astype(BF16)
        o = _dot(a, v)
        outs = []
        for c in range(rows // GLA_CHUNK):
            sl = slice(c * GLA_CHUNK, (c + 1) * GLA_CHUNK)
            s = s_s[...]
            oc = o[sl] + _dot(qg[sl], s.astype(BF16))
            upd = _dot(jnp.where(col_chunk == c, kd_t, 0.0).astype(BF16), v)
            s_s[...] = s * decay_t[:, c * GLA_CHUNK:c * GLA_CHUNK + 1] + upd
            outs.append(_rms(oc, gn))
        return outs

    lead = 2 * GLA_CHUNK - N_META
    pad = lambda ref, w, dt: jnp.concatenate([jnp.zeros((lead, w), dt), ref[0, :N_META, :]], axis=0)
    outs = block(pad(q_ref, GLA_DK, q_ref.dtype), pad(k_ref, GLA_DK, k_ref.dtype),
                 pad(g_ref, GLA_DK, F32), pad(v_ref, GLA_DV, v_ref.dtype), 2 * GLA_CHUNK)
    o_ref[0, :N_META, :] = outs[1][GLA_CHUNK - N_META:].astype(o_ref.dtype)

    def body(n, carry):
        r0 = pl.multiple_of(N_META + n * GLA_BLOCK, N_META)
        rows = pl.ds(r0, GLA_BLOCK)
        outs = block(q_ref[0, rows, :], k_ref[0, rows, :], g_ref[0, rows, :], v_ref[0, rows, :], GLA_BLOCK)
        o_ref[0, rows, :] = jnp.concatenate(outs, axis=0).astype(o_ref.dtype)
        return carry

    lax.fori_loop(0, n_blocks, body, 0)
    st_ref[0, 0] = s_s[...]


def _gla_prompt(gq, gk, gl, gv, gn):
    b, seq, _ = gq.shape
    assert (seq - N_META) % GLA_BLOCK == 0
    idx = lax.broadcasted_iota(jnp.int32, (GLA_BLOCK, GLA_BLOCK), 0), lax.broadcasted_iota(jnp.int32, (GLA_BLOCK, GLA_BLOCK), 1)
    same = idx[0] // GLA_CHUNK == idx[1] // GLA_CHUNK
    tri = (same & (idx[1] <= idx[0])).astype(BF16)
    ones = same.astype(BF16)
    kspec = pl.BlockSpec((1, seq, GLA_DK), lambda i, j: (i, 0, j))
    vspec = pl.BlockSpec((1, seq, GLA_DV), lambda i, j: (i, 0, j))
    return pl.pallas_call(
        functools.partial(_gla_prompt_kernel, seq=seq), grid=(b, GLA_HEADS),
        in_specs=[kspec, kspec, kspec, vspec, _const_spec((1, GLA_DV)),
                  _const_spec((GLA_BLOCK, GLA_BLOCK)), _const_spec((GLA_BLOCK, GLA_BLOCK))],
        out_specs=[vspec, pl.BlockSpec((1, 1, GLA_DK, GLA_DV), lambda i, j: (i, j, 0, 0))],
        out_shape=[jax.ShapeDtypeStruct((b, seq, GLA_VW), BF16),
                   jax.ShapeDtypeStruct((b, GLA_HEADS, GLA_DK, GLA_DV), F32)],
        scratch_shapes=[pltpu.VMEM((GLA_DK, GLA_DV), F32)],
        compiler_params=_params("parallel", "parallel"), name="gla_prompt")(gq, gk, gl, gv, gn, tri, ones)


def _merge_kernel(x_ref, fo_ref, go_ref, sr_ref, ma_ref, mb_ref, wa_ref, wb_ref, wo_ref, gpost_ref, gpre_ref,
                  x1_ref, h2_ref):
    gated = (go_ref[...].astype(F32) * sr_ref[...].astype(F32)).astype(BF16)
    y = (ma_ref[...].astype(F32) * _dot(fo_ref[...].astype(BF16), wa_ref[...])
         + mb_ref[...].astype(F32) * _dot(gated, wb_ref[...]))
    mix = _dot(y.astype(BF16), wo_ref[...])
    x1 = x_ref[...] + _rms(mix, gpost_ref[...])
    x1_ref[...] = x1
    h2_ref[...] = _rms(x1, gpre_ref[...]).astype(BF16)


def _merge(x, fo, go, sr, ma, mb, wa, wb, wo, gpost, gpre, *, tm):
    n = x.shape[0]
    row = lambda w: pl.BlockSpec((tm, w), lambda i: (i, 0))
    return pl.pallas_call(
        _merge_kernel, grid=(n // tm,),
        in_specs=[row(D_MODEL), row(FOX_W), row(GLA_VW), row(GLA_VW), row(D_MODEL), row(D_MODEL),
                  _const_spec((FOX_W, D_MODEL)), _const_spec((GLA_VW, D_MODEL)), _const_spec((D_MODEL, D_MODEL)),
                  _const_spec((1, D_MODEL)), _const_spec((1, D_MODEL))],
        out_specs=[row(D_MODEL), row(D_MODEL)],
        out_shape=[jax.ShapeDtypeStruct((n, D_MODEL), F32), jax.ShapeDtypeStruct((n, D_MODEL), BF16)],
        compiler_params=_params("parallel"), name="merge")(x, fo, go, sr, ma, mb, wa, wb, wo, gpost, gpre)


def _gelu(x):
    return 0.5 * x * (1.0 + jnp.tanh(0.7978845608028654 * (x + 0.044715 * x * x * x)))


def _ffn_kernel(h_ref, x_ref, wup_ref, cw_ref, cb_ref, wdn_ref, g_ref, *rest, tiles_per_seq):
    if tiles_per_seq:
        y_ref, conv_ref, u_s, carry_s = rest
    else:
        p2_ref, p1_ref, y_ref, unew_ref = rest
    h = h_ref[...]
    tm = h.shape[0]
    if tiles_per_seq:
        @pl.when(pl.program_id(0) % tiles_per_seq == 0)
        def _():
            carry_s[...] = jnp.zeros_like(carry_s)

    acc = jnp.zeros((tm, D_MODEL), F32)
    for c in range(D_FF // FFN_COLS):
        halves = []
        for col in (c * FFN_COLS, D_FF + c * FFN_COLS):
            cols = slice(col, col + FFN_COLS)
            u = _dot(h, wup_ref[:, cols])
            if tiles_per_seq:
                u_s[0:8, :] = carry_s[:, cols]
                u_s[8:8 + tm, :] = u
                prev2, prev1 = u_s[6:6 + tm, :], u_s[7:7 + tm, :]
                carry_s[:, cols] = u_s[tm:tm + 8, :]
                conv_ref[0, :, cols] = u_s[tm + 6:tm + 8, :]
            else:
                prev2, prev1 = p2_ref[:, cols], p1_ref[:, cols]
                unew_ref[:, cols] = u
            halves.append(cb_ref[:, cols] + prev2 * cw_ref[0:1, cols] + prev1 * cw_ref[1:2, cols]
                          + u * cw_ref[2:3, cols])
        val, gate = halves
        acc = acc + _dot((_gelu(gate) * val).astype(BF16), wdn_ref[c * FFN_COLS:(c + 1) * FFN_COLS, :])
    y_ref[...] = x_ref[...] + _rms(acc, g_ref[...])


def _ffn(h, x, wup, cw, cb, wdn, g, *, tm, tiles_per_seq, prev=None):
    n = h.shape[0]
    row = lambda w: pl.BlockSpec((tm, w), lambda i: (i, 0))
    in_specs = [row(D_MODEL), row(D_MODEL), _const_spec((D_MODEL, 2 * D_FF)), _const_spec((CONV_W, 2 * D_FF)),
                _const_spec((1, 2 * D_FF)), _const_spec((D_FF, D_MODEL)), _const_spec((1, D_MODEL))]
    args = [h, x, wup, cw, cb, wdn, g]
    if tiles_per_seq:
        n_seq = n // (tm * tiles_per_seq)
        out_shape = [jax.ShapeDtypeStruct((n, D_MODEL), F32), jax.ShapeDtypeStruct((n_seq, CONV_W - 1, 2 * D_FF), F32)]
        out_specs = [row(D_MODEL), pl.BlockSpec((1, CONV_W - 1, 2 * D_FF), lambda i: (i // tiles_per_seq, 0, 0))]
        scratch = [pltpu.VMEM((tm + 8, FFN_COLS), F32), pltpu.VMEM((8, 2 * D_FF), F32)]
    else:
        args += list(prev)
        in_specs += [row(2 * D_FF), row(2 * D_FF)]
        out_shape = [jax.ShapeDtypeStruct((n, D_MODEL), F32), jax.ShapeDtypeStruct((n, 2 * D_FF), F32)]
        out_specs = [row(D_MODEL), row(2 * D_FF)]
        scratch = []
    return pl.pallas_call(
        functools.partial(_ffn_kernel, tiles_per_seq=tiles_per_seq), grid=(n // tm,),
        in_specs=in_specs, out_specs=out_specs, out_shape=out_shape, scratch_shapes=scratch,
        compiler_params=_params("arbitrary"), name="conv_ffn")(*args)


def _fox_sample_kernel(pt_ref, q_ref, kn_ref, vn_ref, lfn_ref, uo_ref, w_ref, *rest):
    del pt_ref
    npg = PAGES_PER_STEP
    k_refs, v_refs, l_refs = rest[:npg], rest[npg:2 * npg], rest[2 * npg:3 * npg]
    o_ref, qc_s, m_s, l_s, acc_s, carry_s, cn_s, row_s = rest[3 * npg:]
    g = pl.program_id(1)
    eye = lax.broadcasted_iota(jnp.int32, (LANES, LANES), 0) == lax.broadcasted_iota(jnp.int32, (LANES, LANES), 1)
    to_col = lambda r: jnp.sum(jnp.where(eye, r, 0.0), axis=1, keepdims=True)
    to_row = lambda c: jnp.sum(jnp.where(eye, c, 0.0), axis=0, keepdims=True)
    sub = lax.broadcasted_iota(jnp.int32, (FOX_HEADS, LANES), 0)
    lane0 = lax.broadcasted_iota(jnp.int32, (FOX_DH, LANES), 1) == 0

    @pl.when(g == 0)
    def _():
        row_s[...] = jnp.zeros_like(row_s)
        row_s[0:1, 0:FOX_HEADS] = lfn_ref[0]
        cn_s[...] = jnp.broadcast_to(to_col(row_s[0:1, :])[:FOX_HEADS], (FOX_HEADS, LANES))
        carry_s[...] = jnp.zeros_like(carry_s)
        s_new = jnp.zeros((FOX_HEADS, LANES), F32)
        for j in range(FOX_HEADS // 2):
            lanes = slice(j * LANES, (j + 1) * LANES)
            qcol = to_col(q_ref[0, :, lanes])
            qc_s[lanes, :] = jnp.broadcast_to(qcol, (LANES, LANES))
            qk = qcol * to_col(kn_ref[0, :, lanes])
            vcol = to_col(vn_ref[0, :, lanes])
            for e in range(2):
                h = 2 * j + e
                rows = slice(e * FOX_DH, (e + 1) * FOX_DH)
                s_new = jnp.where(sub == h, jnp.sum(qk[rows], axis=0, keepdims=True), s_new)
                acc_s[h] = jnp.where(lane0, vcol[rows], 0.0)
        m_s[...] = s_new
        l_s[...] = jnp.ones_like(l_s)

    lf = jnp.concatenate([l_refs[i][0] for i in range(npg)], axis=0)
    both = _dot(_split3_cat(lf), uo_ref[...])
    in_page, page_tot = both[:, :LANES], both[:, LANES:]
    later = _sum3(_dot(w_ref[...], _split3_cat(page_tot)), LANES)
    bias = in_page + later + jnp.concatenate([carry_s[...] + cn_s[...]] * npg, axis=0)
    carry_s[...] = carry_s[...] + later[:FOX_HEADS] + page_tot[:FOX_HEADS]

    pages = [jnp.zeros((FOX_HEADS, LANES), F32)] * npg
    for h in range(FOX_HEADS):
        qc = qc_s[h * FOX_DH:(h + 1) * FOX_DH, :]
        for i in range(npg):
            pages[i] = jnp.where(sub == h, jnp.sum(qc * k_refs[i][0, h], axis=0, keepdims=True), pages[i])
    s = jnp.concatenate(pages, axis=0) + bias
    per_head = lambda x: x.reshape(npg, FOX_HEADS, LANES)
    m_prev = m_s[...]
    m_next = jnp.maximum(m_prev, jnp.max(jnp.max(per_head(s), axis=0), axis=1, keepdims=True))
    p = jnp.exp(s - jnp.concatenate([m_next] * npg, axis=0))
    alpha = jnp.exp(m_prev - m_next)
    l_s[...] = alpha * l_s[...] + jnp.sum(jnp.sum(per_head(p), axis=0), axis=1, keepdims=True)
    m_s[...] = m_next
    for h in range(FOX_HEADS):
        acc = acc_s[h] * alpha[h:h + 1, :]
        for i in range(npg):
            r = i * FOX_HEADS + h
            acc = acc + p[r:r + 1, :] * v_refs[i][0, h]
        acc_s[h] = acc

    @pl.when(g == pl.num_programs(1) - 1)
    def _():
        for j in range(FOX_HEADS // 2):
            cols = [jnp.sum(acc_s[2 * j + e], axis=1, keepdims=True) / l_s[2 * j + e:2 * j + e + 1, 0:1]
                    for e in range(2)]
            o_ref[0, :, j * LANES:(j + 1) * LANES] = to_row(jnp.concatenate(cols, axis=0))


def _fox_sample(q, kn, vn, lfn, cache_kt, cache_vt, cache_lt, page_table):
    db = q.shape[0]
    n_pages = page_table.shape[1]
    npg = PAGES_PER_STEP
    assert n_pages % npg == 0 and npg * FOX_HEADS == LANES
    groups = n_pages // npg
    tok = jnp.arange(PAGE_SIZE)
    after = (tok[:, None] > tok[None, :]).astype(BF16)
    uo = jnp.concatenate([jnp.concatenate([after, jnp.ones_like(after)], axis=1)] * 3, axis=0)
    r = jnp.arange(npg * FOX_HEADS)
    w = ((r[:, None] % FOX_HEADS == r[None, :] % FOX_HEADS) & (r[None, :] // FOX_HEADS > r[:, None] // FOX_HEADS)).astype(BF16)

    def page_spec(i, shape):
        return pl.BlockSpec((1,) + shape, lambda b, g, pt: (pt[b, (groups - 1 - g) * npg + i],) + (0,) * len(shape))

    vec = lambda width: pl.BlockSpec((1, 1, width), lambda b, g, pt: (b, 0, 0))
    const = lambda shape: pl.BlockSpec(shape, lambda b, g, pt: (0,) * len(shape))
    kv_page = (FOX_HEADS, FOX_DH, PAGE_SIZE)
    grid_spec = pltpu.PrefetchScalarGridSpec(
        num_scalar_prefetch=1, grid=(db, groups),
        in_specs=[vec(FOX_W), vec(FOX_W), vec(FOX_W), vec(FOX_HEADS), const(uo.shape), const(w.shape)]
        + [page_spec(i, kv_page) for i in range(npg)] * 2 + [page_spec(i, (FOX_HEADS, PAGE_SIZE)) for i in range(npg)],
        out_specs=vec(FOX_W),
        scratch_shapes=[pltpu.VMEM((FOX_W, LANES), F32), pltpu.VMEM((FOX_HEADS, LANES), F32),
                        pltpu.VMEM((FOX_HEADS, LANES), F32), pltpu.VMEM((FOX_HEADS, FOX_DH, LANES), F32),
                        pltpu.VMEM((FOX_HEADS, LANES), F32), pltpu.VMEM((FOX_HEADS, LANES), F32),
                        pltpu.VMEM((8, LANES), F32)])
    return pl.pallas_call(
        _fox_sample_kernel, grid_spec=grid_spec, out_shape=jax.ShapeDtypeStruct((db, 1, FOX_W), F32),
        compiler_params=_params("parallel", "arbitrary"), name="fox_sample")(
            page_table, q, kn, vn, lfn, uo, w, *([cache_kt] * npg), *([cache_vt] * npg), *([cache_lt] * npg))


def _gla_sample_kernel(q_ref, k_ref, g_ref, v_ref, s_ref, gn_ref, o_ref, sn_ref):
    eye = lax.broadcasted_iota(jnp.int32, (GLA_DK, GLA_DK), 0) == lax.broadcasted_iota(jnp.int32, (GLA_DK, GLA_DK), 1)
    col = lambda r: jnp.sum(jnp.where(eye, r, 0.0), axis=1, keepdims=True)
    scale = GLA_DK ** -0.5
    for h in range(GLA_HEADS):
        ks = slice(h * GLA_DK, (h + 1) * GLA_DK)
        vs = slice(h * GLA_DV, (h + 1) * GLA_DV)
        s = s_ref[0, h] * col(jnp.exp(g_ref[0, :, ks])) + col(k_ref[0, :, ks]) * v_ref[0, :, vs]
        sn_ref[0, h] = s
        qc = col(q_ref[0, :, ks] * scale).astype(BF16).astype(F32)
        o = jnp.sum(qc * s.astype(BF16).astype(F32), axis=0, keepdims=True)
        o_ref[0, :, vs] = _rms(o, gn_ref[...])


def _gla_sample(gq, gk, gl, gv, state, gn):
    db = gq.shape[0]
    kspec = pl.BlockSpec((1, 1, GLA_KW), lambda b: (b, 0, 0))
    vspec = pl.BlockSpec((1, 1, GLA_VW), lambda b: (b, 0, 0))
    sspec = pl.BlockSpec((1, GLA_HEADS, GLA_DK, GLA_DV), lambda b: (b, 0, 0, 0))
    return pl.pallas_call(
        _gla_sample_kernel, grid=(db,),
        in_specs=[kspec, kspec, kspec, vspec, sspec, pl.BlockSpec((1, GLA_DV), lambda b: (0, 0))],
        out_specs=[vspec, sspec],
        out_shape=[jax.ShapeDtypeStruct((db, 1, GLA_VW), F32), jax.ShapeDtypeStruct(state.shape, F32)],
        compiler_params=_params("parallel"), name="gla_sample")(gq, gk, gl, gv, state, gn)


def _layer_weights(l, w_in, b_forget, w_gla_gate_up, b_gla_gate):
    w = w_in[l]
    o = 0
    parts = {}
    for name, width in (("fq", FOX_W), ("fk", FOX_W), ("fv", FOX_W), ("ff", FOX_HEADS), ("gq", GLA_KW), ("gk", GLA_KW),
                        ("gv", GLA_VW), ("gr", GLA_VW), ("ga", GLA_RANK), ("ma", D_MODEL), ("mb", D_MODEL)):
        parts[name] = w[:, o:o + width]
        o += width
    bf = lambda a: a.astype(BF16)
    pad_cols = lambda a: jnp.pad(a, ((0, 0), (0, LANES - a.shape[1])))
    fox = (bf(parts["fq"] * FOX_DH ** -0.5), bf(parts["fk"]), bf(parts["fv"]), bf(pad_cols(parts["ff"])),
           pad_cols(b_forget[l][None, :].astype(F32)))
    wu = jnp.pad(w_gla_gate_up[l], ((0, LANES - GLA_RANK), (0, 0)))
    gla = (bf(parts["gq"]), bf(parts["gk"]), bf(parts["gv"]), bf(parts["gr"]), bf(pad_cols(parts["ga"])), bf(wu),
           b_gla_gate[l][None, :].astype(F32), bf(parts["ma"]), bf(parts["mb"]))
    return fox, gla


def kernel(x_prompt, x_sample, cache_k, cache_v, cache_logf, state_gla, state_conv, page_table, meta_tokens, g_pre_mix, w_in, b_forget, w_gla_gate_up, b_gla_gate, gla_norm_g, w_fox_up, w_gla_up, w_out, g_post_mix, g_pre_ffn, w_up, conv_w, conv_b, w_down, g_post_ffn):
    bsz, seq0, _ = x_prompt.shape
    db, ds, _ = x_sample.shape
    depth = w_in.shape[0]
    assert depth == 1 and ds == 1
    seq = N_META + seq0
    tiles_per_seq = 3
    tm_seq = seq // tiles_per_seq
    assert tm_seq * tiles_per_seq == seq and tm_seq % 16 == 0
    n_p = bsz * seq
    tm_row = 512
    assert n_p % tm_row == 0
    n_pool = cache_k.shape[1]
    l = 0
    row1 = lambda a: a[l][None, :].astype(F32)

    fox_w, gla_w = _layer_weights(l, w_in, b_forget, w_gla_gate_up, b_gla_gate)
    g_pre, gn = row1(g_pre_mix), row1(gla_norm_g)
    wa, wb, wo = w_fox_up[l].astype(BF16), w_gla_up[l].astype(BF16), w_out[l].astype(BF16)
    g_post, g_ffn_pre, g_ffn_post = row1(g_post_mix), row1(g_pre_ffn), row1(g_post_ffn)
    wup, wdn = w_up[l].astype(BF16), w_down[l].astype(BF16)
    cw, cb = conv_w[l].astype(F32), row1(conv_b)

    meta = jnp.broadcast_to(meta_tokens.astype(x_prompt.dtype), (bsz, N_META, D_MODEL))
    xp = jnp.concatenate([meta, x_prompt], axis=1).reshape(n_p, D_MODEL)
    fq, fk, fv, lf, cq, ck = _fox_proj(xp, g_pre, *fox_w, tm=tm_seq, tiles_per_seq=tiles_per_seq, q_dtype=BF16)
    gq, gk, gv, gl, sr, ma, mb = _gla_proj(xp, g_pre, *gla_w, tm=tm_row, mid_dtype=BF16)
    s3 = lambda a: a.reshape(bsz, seq, a.shape[-1])
    fo = _fox_prompt(s3(fq), s3(fk), s3(fv), s3(cq), s3(ck))
    go, gla_p = _gla_prompt(s3(gq), s3(gk), s3(gl), s3(gv), gn)
    x1, h2 = _merge(xp, fo.reshape(n_p, FOX_W), go.reshape(n_p, GLA_VW), sr, ma, mb, wa, wb, wo, g_post, g_ffn_pre,
                    tm=tm_row)
    yp, conv_p = _ffn(h2, x1, wup, cw, cb, wdn, g_ffn_post, tm=tm_seq, tiles_per_seq=tiles_per_seq)
    y_prompt = yp.reshape(bsz, seq, D_MODEL)[:, N_META:]

    xs = x_sample.reshape(db, D_MODEL)
    sq, sk, sv, slf = _fox_proj(xs, g_pre, *fox_w, tm=db, tiles_per_seq=0, q_dtype=F32)
    tq, tk, tv, tl, tsr, tma, tmb = _gla_proj(xs, g_pre, *gla_w, tm=db, mid_dtype=F32)
    v3 = lambda a: a.reshape(db, 1, a.shape[-1])
    so = _fox_sample(v3(sq), v3(sk), v3(sv), v3(slf),
                     jnp.transpose(cache_k[l], (0, 2, 3, 1)), jnp.transpose(cache_v[l], (0, 2, 3, 1)),
                     jnp.transpose(cache_logf[l], (0, 2, 1)), page_table)
    sgo, gla_s = _gla_sample(v3(tq), v3(tk), v3(tl), v3(tv), state_gla[l], gn)
    sx1, sh2 = _merge(xs, so.reshape(db, FOX_W), sgo.reshape(db, GLA_VW), tsr, tma, tmb, wa, wb, wo, g_post,
                      g_ffn_pre, tm=db)
    ys, u_new = _ffn(sh2, sx1, wup, cw, cb, wdn, g_ffn_post, tm=db, tiles_per_seq=0,
                     prev=(state_conv[l][:, 0, :], state_conv[l][:, 1, :]))
    conv_s = jnp.stack([state_conv[l][:, 1, :], u_new], axis=1)

    heads = lambda a, n: a.reshape(1, n, -1, FOX_HEADS, FOX_DH)
    return (y_prompt, ys.reshape(db, 1, D_MODEL),
            heads(fk, bsz), heads(fv, bsz), lf.reshape(1, bsz, seq, FOX_HEADS), gla_p[None], conv_p[None],
            heads(sk, db), heads(sv, db), slf.reshape(1, db, 1, FOX_HEADS), gla_s[None], conv_s[None])
```

```python
import functools

import jax
import jax.numpy as jnp
from jax import lax
from jax.experimental import pallas as pl
from jax.experimental.pallas import tpu as pltpu

D_MODEL = 1024
N_META = 16
FOX_HEADS = 8
FOX_DH = 64
FOX_W = FOX_HEADS * FOX_DH
GLA_HEADS = 4
GLA_DK = 128
GLA_DV = 256
GLA_KW = GLA_HEADS * GLA_DK
GLA_VW = GLA_HEADS * GLA_DV
GLA_RANK = 16
GLA_TAU = 16.0
GLA_CHUNK = 64
D_FF = 2816
CONV_W = 3
EPS = 1e-6
PAGE_SIZE = 128

LANES = 128
ATT_BLOCK = 256
ATT_ROWS = 1024
GLA_BLOCK = 256
FFN_COLS = 256
PAGES_PER_STEP = 16
NEG = -1e30
VMEM_LIMIT = 56 * 1024 * 1024

F32 = jnp.float32
BF16 = jnp.bfloat16

_NT = (((1,), (1,)), ((), ()))


def _dot(a, b):
    return jnp.dot(a, b, preferred_element_type=F32)


def _dot_nt(a, b):
    return lax.dot_general(a, b, _NT, preferred_element_type=F32)


def _rms(x, g):
    return x * lax.rsqrt(jnp.mean(x * x, axis=-1, keepdims=True) + EPS) * g


def _log_sigmoid(x):
    return jnp.minimum(x, 0.0) - jnp.log1p(jnp.exp(-jnp.abs(x)))


def _sigmoid(x):
    return 1.0 / (1.0 + jnp.exp(-x))


def _split3(x):
    x1 = x.astype(BF16)
    r1 = x - x1.astype(F32)
    x2 = r1.astype(BF16)
    x3 = (r1 - x2.astype(F32)).astype(BF16)
    return x1, x2, x3


def _split3_cat(x):
    return jnp.concatenate(_split3(x), axis=1)


def _sum3(y, w):
    return y[:, :w] + y[:, w:2 * w] + y[:, 2 * w:]


def _exact_left(mat01, x):
    return _sum3(_dot(mat01, _split3_cat(x)), x.shape[1])


def _const_spec(shape):
    return pl.BlockSpec(shape, lambda *_: (0,) * len(shape), pipeline_mode=pl.Buffered(1))


def _params(*sem):
    return pltpu.CompilerParams(dimension_semantics=sem, vmem_limit_bytes=VMEM_LIMIT)


def _fox_proj_kernel(x_ref, g_ref, wq_ref, wk_ref, wv_ref, wf_ref, bf_ref, *rest, tiles_per_seq):
    if tiles_per_seq:
        tri_ref, place_ref, ones_ref, fq_ref, k_ref, v_ref, lf_ref, cq_ref, ck_ref, carry_ref = rest
    else:
        fq_ref, k_ref, v_ref, lf_ref = rest
    h = _rms(x_ref[...], g_ref[...]).astype(BF16)
    fq_ref[...] = _dot(h, wq_ref[...]).astype(fq_ref.dtype)
    k_ref[...] = _dot(h, wk_ref[...])
    v_ref[...] = _dot(h, wv_ref[...])
    lane = lax.broadcasted_iota(jnp.int32, (1, LANES), 1)
    lf = jnp.where(lane < FOX_HEADS, _log_sigmoid(_dot(h, wf_ref[...]) + bf_ref[...]), 0.0)
    lf_ref[...] = lf[:, :FOX_HEADS]
    if tiles_per_seq:
        @pl.when(pl.program_id(0) % tiles_per_seq == 0)
        def _():
            carry_ref[...] = jnp.zeros_like(carry_ref)

        tm = lf.shape[0]
        c = carry_ref[...] + _exact_left(tri_ref[...], lf)
        carry_ref[...] = c[tm - 1:tm, :]
        aug = _dot(_split3_cat(c), place_ref[...]) + ones_ref[...]
        cq_ref[...] = aug[:, :FOX_W].astype(BF16)
        ck_ref[...] = aug[:, FOX_W:].astype(BF16)


def _placement():
    import numpy as np
    place = np.zeros((3 * LANES, 2 * FOX_W), np.float32)
    ones = np.zeros((1, 2 * FOX_W), np.float32)
    for h in range(FOX_HEADS):
        base = LANES * (h // 2) + (FOX_DH if h % 2 == 0 else 0)
        for t in range(3):
            place[LANES * t + h, base + t] = 1.0
            ones[0, base + 3 + t] = 1.0
            ones[0, FOX_W + base + t] = 1.0
            place[LANES * t + h, FOX_W + base + 3 + t] = -1.0
    return jnp.asarray(place, BF16), jnp.asarray(ones, F32)


def _fox_proj(x, g, wq, wk, wv, wf, bf, *, tm, tiles_per_seq, q_dtype):
    n = x.shape[0]
    row = lambda w: pl.BlockSpec((tm, w), lambda i: (i, 0))
    in_specs = [row(D_MODEL), _const_spec((1, D_MODEL)), _const_spec((D_MODEL, FOX_W)),
                _const_spec((D_MODEL, FOX_W)), _const_spec((D_MODEL, FOX_W)),
                _const_spec((D_MODEL, LANES)), _const_spec((1, LANES))]
    out_shape = [jax.ShapeDtypeStruct((n, FOX_W), q_dtype), jax.ShapeDtypeStruct((n, FOX_W), F32),
                 jax.ShapeDtypeStruct((n, FOX_W), F32), jax.ShapeDtypeStruct((n, FOX_HEADS), F32)]
    out_specs = [row(FOX_W), row(FOX_W), row(FOX_W), row(FOX_HEADS)]
    args = [x, g, wq, wk, wv, wf, bf]
    scratch = []
    if tiles_per_seq:
        place, ones = _placement()
        tri = jnp.tril(jnp.ones((tm, tm), BF16))
        args += [tri, place, ones]
        in_specs += [_const_spec((tm, tm)), _const_spec((3 * LANES, 2 * FOX_W)), _const_spec((1, 2 * FOX_W))]
        out_shape += [jax.ShapeDtypeStruct((n, FOX_W), BF16)] * 2
        out_specs += [row(FOX_W), row(FOX_W)]
        scratch = [pltpu.VMEM((1, LANES), F32)]
    return pl.pallas_call(
        functools.partial(_fox_proj_kernel, tiles_per_seq=tiles_per_seq),
        grid=(n // tm,), in_specs=in_specs, out_specs=out_specs, out_shape=out_shape,
        scratch_shapes=scratch, compiler_params=_params("arbitrary"), name="fox_proj")(*args)


def _gla_proj_kernel(x_ref, g_ref, wq_ref, wk_ref, wv_ref, wr_ref, wa_ref, wu_ref, bu_ref, wma_ref, wmb_ref,
                     gq_ref, gk_ref, gv_ref, gl_ref, sr_ref, ma_ref, mb_ref):
    h = _rms(x_ref[...], g_ref[...]).astype(BF16)
    gq_ref[...] = _dot(h, wq_ref[...]).astype(gq_ref.dtype)
    gk_ref[...] = _dot(h, wk_ref[...]).astype(gk_ref.dtype)
    gv_ref[...] = _dot(h, wv_ref[...]).astype(gv_ref.dtype)
    low = _dot(h, wa_ref[...]).astype(BF16)
    gl_ref[...] = _log_sigmoid(_dot(low, wu_ref[...]) + bu_ref[...]) * (1.0 / GLA_TAU)
    r = _dot(h, wr_ref[...])
    sr_ref[...] = (r * _sigmoid(r)).astype(sr_ref.dtype)
    ma_ref[...] = _sigmoid(_dot(h, wma_ref[...])).astype(ma_ref.dtype)
    mb_ref[...] = _sigmoid(_dot(h, wmb_ref[...])).astype(mb_ref.dtype)


def _gla_proj(x, g, wq, wk, wv, wr, wa, wu, bu, wma, wmb, *, tm, mid_dtype):
    n = x.shape[0]
    row = lambda w: pl.BlockSpec((tm, w), lambda i: (i, 0))
    widths = [GLA_KW, GLA_KW, GLA_VW, GLA_KW, GLA_VW, D_MODEL, D_MODEL]
    dtypes = [mid_dtype, mid_dtype, mid_dtype, F32, mid_dtype, mid_dtype, mid_dtype]
    return pl.pallas_call(
        _gla_proj_kernel, grid=(n // tm,),
        in_specs=[row(D_MODEL), _const_spec((1, D_MODEL)), _const_spec((D_MODEL, GLA_KW)),
                  _const_spec((D_MODEL, GLA_KW)), _const_spec((D_MODEL, GLA_VW)), _const_spec((D_MODEL, GLA_VW)),
                  _const_spec((D_MODEL, LANES)), _const_spec((LANES, GLA_KW)), _const_spec((1, GLA_KW)),
                  _const_spec((D_MODEL, D_MODEL)), _const_spec((D_MODEL, D_MODEL))],
        out_specs=[row(w) for w in widths],
        out_shape=[jax.ShapeDtypeStruct((n, w), dt) for w, dt in zip(widths, dtypes)],
        compiler_params=_params("parallel"), name="gla_proj")(x, g, wq, wk, wv, wr, wa, wu, bu, wma, wmb)


def _fox_prompt_kernel(fq_ref, k_ref, v_ref, cq_ref, ck_ref, o_ref, kt_ref, vt_ref, q_s, k_s, v_s, m_s, l_s, acc_s,
                       *, seq):
    n_full = (seq - N_META) // ATT_BLOCK
    full = n_full * ATT_BLOCK
    lane = lax.broadcasted_iota(jnp.int32, (1, LANES), 1)
    lo = lane < FOX_DH
    qp, cq = fq_ref[0], cq_ref[0]
    q_s[0] = jnp.where(lo, qp, cq)
    q_s[1] = jnp.where(lo, cq, qp)
    kp, ck = k_ref[0].astype(BF16), ck_ref[0]
    k_s[0] = jnp.where(lo, kp, ck)
    k_s[1] = jnp.where(lo, ck, kp)
    v_s[...] = v_ref[0].astype(BF16)

    for src, dst in ((k_ref, kt_ref), (v_ref, vt_ref)):
        dst[0, :, :, :full] = src[0, :full, :].T.reshape(2, FOX_DH, full)
        last_t = src[0, seq - LANES:seq, :].T
        dst[0, :, :, full:seq] = last_t[:, LANES - N_META:].reshape(2, FOX_DH, N_META)

    r_io = lax.broadcasted_iota(jnp.int32, (ATT_BLOCK, ATT_BLOCK), 0)
    c_io = lax.broadcasted_iota(jnp.int32, (ATT_BLOCK, ATT_BLOCK), 1)
    diag_mask = c_io <= r_io
    wide = lambda x: jnp.concatenate([x] * (ATT_BLOCK // LANES), axis=1)

    for j in range(n_full):
        k0 = j * ATT_BLOCK
        keys = slice(k0, k0 + ATT_BLOCK)
        for r0 in range(k0, full, ATT_ROWS):
            rows = slice(r0, min(r0 + ATT_ROWS, full))
            for e in range(2):
                s = _dot_nt(q_s[e, rows, :], k_s[e, keys, :])
                if r0 == k0:
                    top = jnp.where(diag_mask, s[:ATT_BLOCK], NEG)
                    s = top if s.shape[0] == ATT_BLOCK else jnp.concatenate([top, s[ATT_BLOCK:]], axis=0)
                m_cur = jnp.max(s, axis=1, keepdims=True)
                if j == 0:
                    m_next = jnp.broadcast_to(m_cur, (s.shape[0], LANES))
                    p = jnp.exp(s - wide(m_next))
                    l_s[e, rows] = jnp.broadcast_to(jnp.sum(p, axis=1, keepdims=True), m_next.shape)
                    acc_s[e, rows] = _dot(p.astype(BF16), v_s[keys, :])
                else:
                    m_prev = m_s[e, rows]
                    m_next = jnp.maximum(m_prev, m_cur)
                    p = jnp.exp(s - wide(m_next))
                    alpha = jnp.exp(m_prev - m_next)
                    l_s[e, rows] = alpha * l_s[e, rows] + jnp.sum(p, axis=1, keepdims=True)
                    acc_s[e, rows] = alpha * acc_s[e, rows] + _dot(p.astype(BF16), v_s[keys, :])
                m_s[e, rows] = m_next
    o_ref[0, :full, :] = jnp.where(lo, acc_s[0] / l_s[0], acc_s[1] / l_s[1]).astype(o_ref.dtype)

    tr = lax.broadcasted_iota(jnp.int32, (N_META, ATT_BLOCK), 0)
    tc = lax.broadcasted_iota(jnp.int32, (N_META, ATT_BLOCK), 1)
    first_new = ATT_BLOCK - N_META
    tail_mask = (tc >= first_new) & (tc - first_new <= tr)
    last = slice(seq - ATT_BLOCK, seq)
    outs = []
    for e in range(2):
        q = q_s[e, full:seq, :]
        s_old = _dot_nt(q, k_s[e, :full, :])
        s_new = jnp.where(tail_mask, _dot_nt(q, k_s[e, last, :]), NEG)
        m = jnp.maximum(jnp.max(s_old, axis=1, keepdims=True), jnp.max(s_new, axis=1, keepdims=True))
        p_old, p_new = jnp.exp(s_old - m), jnp.exp(s_new - m)
        l = jnp.sum(p_old, axis=1, keepdims=True) + jnp.sum(p_new, axis=1, keepdims=True)
        outs.append((_dot(p_old.astype(BF16), v_s[:full, :]) + _dot(p_new.astype(BF16), v_s[last, :])) / l)
    o_ref[0, full:seq, :] = jnp.where(lo, outs[0], outs[1]).astype(o_ref.dtype)


def _fox_prompt(fq, k, v, cq, ck):
    b, seq, _ = fq.shape
    assert (seq - N_META) % ATT_BLOCK == 0 and seq >= ATT_BLOCK
    full = seq - N_META
    spec = pl.BlockSpec((1, seq, LANES), lambda i, j: (i, 0, j))
    t_spec = pl.BlockSpec((1, 2, FOX_DH, seq), lambda i, j: (i, j, 0, 0))
    t_shape = jax.ShapeDtypeStruct((b, FOX_HEADS, FOX_DH, seq), F32)
    return pl.pallas_call(
        functools.partial(_fox_prompt_kernel, seq=seq), grid=(b, FOX_W // LANES),
        in_specs=[spec] * 5, out_specs=[spec, t_spec, t_spec],
        out_shape=[jax.ShapeDtypeStruct((b, seq, FOX_W), BF16), t_shape, t_shape],
        scratch_shapes=[pltpu.VMEM((2, seq, LANES), BF16), pltpu.VMEM((2, seq, LANES), BF16),
                        pltpu.VMEM((seq, LANES), BF16), pltpu.VMEM((2, full, LANES), F32),
                        pltpu.VMEM((2, full, LANES), F32), pltpu.VMEM((2, full, LANES), F32)],
        compiler_params=_params("parallel", "parallel"), name="fox_prompt")(fq, k, v, cq, ck)


def _gla_prompt_kernel(q_ref, k_ref, g_ref, v_ref, gn_ref, tri_ref, ones_ref, o_ref, st_ref, s_s, *, seq):
    n_blocks = (seq - N_META) // GLA_BLOCK
    scale = GLA_DK ** -0.5
    gn = gn_ref[...]
    s_s[...] = jnp.zeros_like(s_s)

    def block(h, q, k, g, v, rows):
        r_io = lax.broadcasted_iota(jnp.int32, (rows, rows), 0)
        c_io = lax.broadcasted_iota(jnp.int32, (rows, rows), 1)
        intra = (r_io // GLA_CHUNK == c_io // GLA_CHUNK) & (c_io <= r_io)
        col_chunk = lax.broadcasted_iota(jnp.int32, (GLA_DK, rows), 1) // GLA_CHUNK
        tri = tri_ref[:rows, :rows]
        ones = ones_ref[:rows, :rows]
        g3 = _split3_cat(g)
        b = _sum3(_dot(tri, g3), GLA_DK)
        b_last = _sum3(_dot(ones, g3), GLA_DK)
        kf = k.astype(F32)
        qg = (q.astype(F32) * (scale * jnp.exp(b))).astype(BF16)
        kb = (kf * jnp.exp(-b)).astype(BF16)
        kd_t = (kf * jnp.exp(b_last - b)).T
        decay_t = jnp.exp(b_last).T
        a = jnp.where(intra, _dot_nt(qg, kb), 0.0).astype(BF16)
        o = _dot(a, v)
        outs = []
        for c in range(rows // GLA_CHUNK):
            sl = slice(c * GLA_CHUNK, (c + 1) * GLA_CHUNK)
            s = s_s[h]
            oc = o[sl] + _dot(qg[sl], s.astype(BF16))
            upd = _dot(jnp.where(col_chunk == c, kd_t, 0.0).astype(BF16), v)
            s_s[h] = s * decay_t[:, c * GLA_CHUNK:c * GLA_CHUNK + 1] + upd
            outs.append(_rms(oc, gn))
        return outs

    ks = lambda h: slice(h * GLA_DK, (h + 1) * GLA_DK)
    vs = lambda h: slice(h * GLA_DV, (h + 1) * GLA_DV)

    lead = 2 * GLA_CHUNK - N_META
    pad = lambda ref, cols: jnp.concatenate(
        [jnp.zeros((lead, cols.stop - cols.start), ref.dtype), ref[0, :N_META, cols]], axis=0)
    for h in range(GLA_HEADS):
        outs = block(h, pad(q_ref, ks(h)), pad(k_ref, ks(h)), pad(g_ref, ks(h)), pad(v_ref, vs(h)), 2 * GLA_CHUNK)
        o_ref[0, :N_META, vs(h)] = outs[1][GLA_CHUNK - N_META:].astype(o_ref.dtype)

    def body(n, carry):
        r0 = pl.multiple_of(N_META + n * GLA_BLOCK, N_META)
        rows = pl.ds(r0, GLA_BLOCK)
        for h in range(GLA_HEADS):
            outs = block(h, q_ref[0, rows, ks(h)], k_ref[0, rows, ks(h)], g_ref[0, rows, ks(h)],
                         v_ref[0, rows, vs(h)], GLA_BLOCK)
            o_ref[0, rows, vs(h)] = jnp.concatenate(outs, axis=0).astype(o_ref.dtype)
        return carry

    lax.fori_loop(0, n_blocks, body, 0)
    st_ref[0] = s_s[...]


def _gla_prompt(gq, gk, gl, gv, gn):
    b, seq, _ = gq.shape
    assert (seq - N_META) % GLA_BLOCK == 0
    idx = lax.broadcasted_iota(jnp.int32, (GLA_BLOCK, GLA_BLOCK), 0), lax.broadcasted_iota(jnp.int32, (GLA_BLOCK, GLA_BLOCK), 1)
    same = idx[0] // GLA_CHUNK == idx[1] // GLA_CHUNK
    tri = (same & (idx[1] <= idx[0])).astype(BF16)
    ones = same.astype(BF16)
    kspec = pl.BlockSpec((1, seq, GLA_KW), lambda i: (i, 0, 0))
    vspec = pl.BlockSpec((1, seq, GLA_VW), lambda i: (i, 0, 0))
    state = (GLA_HEADS, GLA_DK, GLA_DV)
    return pl.pallas_call(
        functools.partial(_gla_prompt_kernel, seq=seq), grid=(b,),
        in_specs=[kspec, kspec, kspec, vspec, _const_spec((1, GLA_DV)),
                  _const_spec((GLA_BLOCK, GLA_BLOCK)), _const_spec((GLA_BLOCK, GLA_BLOCK))],
        out_specs=[vspec, pl.BlockSpec((1,) + state, lambda i: (i, 0, 0, 0))],
        out_shape=[jax.ShapeDtypeStruct((b, seq, GLA_VW), BF16), jax.ShapeDtypeStruct((b,) + state, F32)],
        scratch_shapes=[pltpu.VMEM(state, F32)],
        compiler_params=_params("parallel"), name="gla_prompt")(gq, gk, gl, gv, gn, tri, ones)


def _merge_kernel(x_ref, fo_ref, go_ref, sr_ref, ma_ref, mb_ref, wa_ref, wb_ref, wo_ref, gpost_ref, gpre_ref,
                  x1_ref, h2_ref):
    gated = (go_ref[...].astype(F32) * sr_ref[...].astype(F32)).astype(BF16)
    y = (ma_ref[...].astype(F32) * _dot(fo_ref[...].astype(BF16), wa_ref[...])
         + mb_ref[...].astype(F32) * _dot(gated, wb_ref[...]))
    mix = _dot(y.astype(BF16), wo_ref[...])
    x1 = x_ref[...] + _rms(mix, gpost_ref[...])
    x1_ref[...] = x1
    h2_ref[...] = _rms(x1, gpre_ref[...]).astype(BF16)


def _merge(x, fo, go, sr, ma, mb, wa, wb, wo, gpost, gpre, *, tm):
    n = x.shape[0]
    row = lambda w: pl.BlockSpec((tm, w), lambda i: (i, 0))
    return pl.pallas_call(
        _merge_kernel, grid=(n // tm,),
        in_specs=[row(D_MODEL), row(FOX_W), row(GLA_VW), row(GLA_VW), row(D_MODEL), row(D_MODEL),
                  _const_spec((FOX_W, D_MODEL)), _const_spec((GLA_VW, D_MODEL)), _const_spec((D_MODEL, D_MODEL)),
                  _const_spec((1, D_MODEL)), _const_spec((1, D_MODEL))],
        out_specs=[row(D_MODEL), row(D_MODEL)],
        out_shape=[jax.ShapeDtypeStruct((n, D_MODEL), F32), jax.ShapeDtypeStruct((n, D_MODEL), BF16)],
        compiler_params=_params("parallel"), name="merge")(x, fo, go, sr, ma, mb, wa, wb, wo, gpost, gpre)


def _gelu(x):
    return 0.5 * x * (1.0 + jnp.tanh(0.7978845608028654 * (x + 0.044715 * x * x * x)))


def _ffn_kernel(h_ref, x_ref, wup_ref, cw_ref, cb_ref, wdn_ref, g_ref, *rest, tiles_per_seq):
    if tiles_per_seq:
        y_ref, conv_ref, act_s, u_s, carry_s = rest
    else:
        p2_ref, p1_ref, y_ref, unew_ref, act_s = rest
    h = h_ref[...]
    tm = h.shape[0]
    if tiles_per_seq:
        @pl.when(pl.program_id(0) % tiles_per_seq == 0)
        def _():
            carry_s[...] = jnp.zeros_like(carry_s)

    for c in range(D_FF // FFN_COLS):
        halves = []
        for col in (c * FFN_COLS, D_FF + c * FFN_COLS):
            cols = slice(col, col + FFN_COLS)
            u = _dot(h, wup_ref[:, cols])
            if tiles_per_seq:
                u_s[0:8, :] = carry_s[:, cols]
                u_s[8:8 + tm, :] = u
                prev2, prev1 = u_s[6:6 + tm, :], u_s[7:7 + tm, :]
                carry_s[:, cols] = u_s[tm:tm + 8, :]
                conv_ref[0, :, cols] = u_s[tm + 6:tm + 8, :]
            else:
                prev2, prev1 = p2_ref[:, cols], p1_ref[:, cols]
                unew_ref[:, cols] = u
            halves.append(cb_ref[:, cols] + prev2 * cw_ref[0:1, cols] + prev1 * cw_ref[1:2, cols]
                          + u * cw_ref[2:3, cols])
        val, gate = halves
        act_s[:, c * FFN_COLS:(c + 1) * FFN_COLS] = (_gelu(gate) * val).astype(BF16)
    y_ref[...] = x_ref[...] + _rms(_dot(act_s[...], wdn_ref[...]), g_ref[...])


def _ffn(h, x, wup, cw, cb, wdn, g, *, tm, tiles_per_seq, prev=None):
    n = h.shape[0]
    row = lambda w: pl.BlockSpec((tm, w), lambda i: (i, 0))
    in_specs = [row(D_MODEL), row(D_MODEL), _const_spec((D_MODEL, 2 * D_FF)), _const_spec((CONV_W, 2 * D_FF)),
                _const_spec((1, 2 * D_FF)), _const_spec((D_FF, D_MODEL)), _const_spec((1, D_MODEL))]
    args = [h, x, wup, cw, cb, wdn, g]
    if tiles_per_seq:
        n_seq = n // (tm * tiles_per_seq)
        out_shape = [jax.ShapeDtypeStruct((n, D_MODEL), F32), jax.ShapeDtypeStruct((n_seq, CONV_W - 1, 2 * D_FF), F32)]
        out_specs = [row(D_MODEL), pl.BlockSpec((1, CONV_W - 1, 2 * D_FF), lambda i: (i // tiles_per_seq, 0, 0))]
        scratch = [pltpu.VMEM((tm, D_FF), BF16), pltpu.VMEM((tm + 8, FFN_COLS), F32), pltpu.VMEM((8, 2 * D_FF), F32)]
    else:
        args += list(prev)
        in_specs += [row(2 * D_FF), row(2 * D_FF)]
        out_shape = [jax.ShapeDtypeStruct((n, D_MODEL), F32), jax.ShapeDtypeStruct((n, 2 * D_FF), F32)]
        out_specs = [row(D_MODEL), row(2 * D_FF)]
        scratch = [pltpu.VMEM((tm, D_FF), BF16)]
    return pl.pallas_call(
        functools.partial(_ffn_kernel, tiles_per_seq=tiles_per_seq), grid=(n // tm,),
        in_specs=in_specs, out_specs=out_specs, out_shape=out_shape, scratch_shapes=scratch,
        compiler_params=_params("arbitrary"), name="conv_ffn")(*args)


def _fox_sample_kernel(pt_ref, q_ref, kn_ref, vn_ref, lfn_ref, uo_ref, w_ref, *rest):
    del pt_ref
    npg = PAGES_PER_STEP
    k_refs, v_refs, l_refs = rest[:npg], rest[npg:2 * npg], rest[2 * npg:3 * npg]
    o_ref, qc_s, m_s, l_s, acc_s, carry_s, cn_s, row_s = rest[3 * npg:]
    g = pl.program_id(1)
    eye = lax.broadcasted_iota(jnp.int32, (LANES, LANES), 0) == lax.broadcasted_iota(jnp.int32, (LANES, LANES), 1)
    to_col = lambda r: jnp.sum(jnp.where(eye, r, 0.0), axis=1, keepdims=True)
    to_row = lambda c: jnp.sum(jnp.where(eye, c, 0.0), axis=0, keepdims=True)
    sub = lax.broadcasted_iota(jnp.int32, (FOX_HEADS, LANES), 0)
    lane0 = lax.broadcasted_iota(jnp.int32, (FOX_DH, LANES), 1) == 0

    @pl.when(g == 0)
    def _():
        row_s[...] = jnp.zeros_like(row_s)
        row_s[0:1, 0:FOX_HEADS] = lfn_ref[0]
        cn_s[...] = jnp.broadcast_to(to_col(row_s[0:1, :])[:FOX_HEADS], (FOX_HEADS, LANES))
        carry_s[...] = jnp.zeros_like(carry_s)
        s_new = jnp.zeros((FOX_HEADS, LANES), F32)
        for j in range(FOX_HEADS // 2):
            lanes = slice(j * LANES, (j + 1) * LANES)
            qcol = to_col(q_ref[0, :, lanes])
            qc_s[lanes, :] = jnp.broadcast_to(qcol, (LANES, LANES))
            qk = qcol * to_col(kn_ref[0, :, lanes])
            vcol = to_col(vn_ref[0, :, lanes])
            for e in range(2):
                h = 2 * j + e
                rows = slice(e * FOX_DH, (e + 1) * FOX_DH)
                s_new = jnp.where(sub == h, jnp.sum(qk[rows], axis=0, keepdims=True), s_new)
                acc_s[h] = jnp.where(lane0, vcol[rows], 0.0)
        m_s[...] = s_new
        l_s[...] = jnp.ones_like(l_s)

    lf = jnp.concatenate([l_refs[i][0] for i in range(npg)], axis=0)
    both = _dot(_split3_cat(lf), uo_ref[...])
    in_page, page_tot = both[:, :LANES], both[:, LANES:]
    later = _sum3(_dot(w_ref[...], _split3_cat(page_tot)), LANES)
    bias = in_page + later + jnp.concatenate([carry_s[...] + cn_s[...]] * npg, axis=0)
    carry_s[...] = carry_s[...] + later[:FOX_HEADS] + page_tot[:FOX_HEADS]

    pages = [jnp.zeros((FOX_HEADS, LANES), F32)] * npg
    for h in range(FOX_HEADS):
        qc = qc_s[h * FOX_DH:(h + 1) * FOX_DH, :]
        for i in range(npg):
            pages[i] = jnp.where(sub == h, jnp.sum(qc * k_refs[i][0, h], axis=0, keepdims=True), pages[i])
    s = jnp.concatenate(pages, axis=0) + bias
    per_head = lambda x: x.reshape(npg, FOX_HEADS, LANES)
    m_prev = m_s[...]
    m_next = jnp.maximum(m_prev, jnp.max(jnp.max(per_head(s), axis=0), axis=1, keepdims=True))
    p = jnp.exp(s - jnp.concatenate([m_next] * npg, axis=0))
    alpha = jnp.exp(m_prev - m_next)
    l_s[...] = alpha * l_s[...] + jnp.sum(jnp.sum(per_head(p), axis=0), axis=1, keepdims=True)
    m_s[...] = m_next
    for h in range(FOX_HEADS):
        acc = acc_s[h] * alpha[h:h + 1, :]
        for i in range(npg):
            r = i * FOX_HEADS + h
            acc = acc + p[r:r + 1, :] * v_refs[i][0, h]
        acc_s[h] = acc

    @pl.when(g == pl.num_programs(1) - 1)
    def _():
        for j in range(FOX_HEADS // 2):
            cols = [jnp.sum(acc_s[2 * j + e], axis=1, keepdims=True) / l_s[2 * j + e:2 * j + e + 1, 0:1]
                    for e in range(2)]
            o_ref[0, :, j * LANES:(j + 1) * LANES] = to_row(jnp.concatenate(cols, axis=0))


def _fox_sample(q, kn, vn, lfn, cache_kt, cache_vt, cache_lt, page_table):
    db = q.shape[0]
    n_pages = page_table.shape[1]
    npg = PAGES_PER_STEP
    assert n_pages % npg == 0 and npg * FOX_HEADS == LANES
    groups = n_pages // npg
    tok = jnp.arange(PAGE_SIZE)
    after = (tok[:, None] > tok[None, :]).astype(BF16)
    uo = jnp.concatenate([jnp.concatenate([after, jnp.ones_like(after)], axis=1)] * 3, axis=0)
    r = jnp.arange(npg * FOX_HEADS)
    w = ((r[:, None] % FOX_HEADS == r[None, :] % FOX_HEADS) & (r[None, :] // FOX_HEADS > r[:, None] // FOX_HEADS)).astype(BF16)

    def page_spec(i, shape):
        return pl.BlockSpec((1,) + shape, lambda b, g, pt: (pt[b, (groups - 1 - g) * npg + i],) + (0,) * len(shape))

    vec = lambda width: pl.BlockSpec((1, 1, width), lambda b, g, pt: (b, 0, 0))
    const = lambda shape: pl.BlockSpec(shape, lambda b, g, pt: (0,) * len(shape))
    kv_page = (FOX_HEADS, FOX_DH, PAGE_SIZE)
    grid_spec = pltpu.PrefetchScalarGridSpec(
        num_scalar_prefetch=1, grid=(db, groups),
        in_specs=[vec(FOX_W), vec(FOX_W), vec(FOX_W), vec(FOX_HEADS), const(uo.shape), const(w.shape)]
        + [page_spec(i, kv_page) for i in range(npg)] * 2 + [page_spec(i, (FOX_HEADS, PAGE_SIZE)) for i in range(npg)],
        out_specs=vec(FOX_W),
        scratch_shapes=[pltpu.VMEM((FOX_W, LANES), F32), pltpu.VMEM((FOX_HEADS, LANES), F32),
                        pltpu.VMEM((FOX_HEADS, LANES), F32), pltpu.VMEM((FOX_HEADS, FOX_DH, LANES), F32),
                        pltpu.VMEM((FOX_HEADS, LANES), F32), pltpu.VMEM((FOX_HEADS, LANES), F32),
                        pltpu.VMEM((8, LANES), F32)])
    return pl.pallas_call(
        _fox_sample_kernel, grid_spec=grid_spec, out_shape=jax.ShapeDtypeStruct((db, 1, FOX_W), F32),
        compiler_params=_params("parallel", "arbitrary"), name="fox_sample")(
            page_table, q, kn, vn, lfn, uo, w, *([cache_kt] * npg), *([cache_vt] * npg), *([cache_lt] * npg))


def _gla_sample_kernel(q_ref, k_ref, g_ref, v_ref, s_ref, gn_ref, o_ref, sn_ref):
    eye = lax.broadcasted_iota(jnp.int32, (GLA_DK, GLA_DK), 0) == lax.broadcasted_iota(jnp.int32, (GLA_DK, GLA_DK), 1)
    col = lambda r: jnp.sum(jnp.where(eye, r, 0.0), axis=1, keepdims=True)
    scale = GLA_DK ** -0.5
    for h in range(GLA_HEADS):
        ks = slice(h * GLA_DK, (h + 1) * GLA_DK)
        vs = slice(h * GLA_DV, (h + 1) * GLA_DV)
        s = s_ref[0, h] * col(jnp.exp(g_ref[0, :, ks])) + col(k_ref[0, :, ks]) * v_ref[0, :, vs]
        sn_ref[0, h] = s
        qc = col(q_ref[0, :, ks] * scale).astype(BF16).astype(F32)
        o = jnp.sum(qc * s.astype(BF16).astype(F32), axis=0, keepdims=True)
        o_ref[0, :, vs] = _rms(o, gn_ref[...])


def _gla_sample(gq, gk, gl, gv, state, gn):
    db = gq.shape[0]
    kspec = pl.BlockSpec((1, 1, GLA_KW), lambda b: (b, 0, 0))
    vspec = pl.BlockSpec((1, 1, GLA_VW), lambda b: (b, 0, 0))
    sspec = pl.BlockSpec((1, GLA_HEADS, GLA_DK, GLA_DV), lambda b: (b, 0, 0, 0))
    return pl.pallas_call(
        _gla_sample_kernel, grid=(db,),
        in_specs=[kspec, kspec, kspec, vspec, sspec, pl.BlockSpec((1, GLA_DV), lambda b: (0, 0))],
        out_specs=[vspec, sspec],
        out_shape=[jax.ShapeDtypeStruct((db, 1, GLA_VW), F32), jax.ShapeDtypeStruct(state.shape, F32)],
        compiler_params=_params("parallel"), name="gla_sample")(gq, gk, gl, gv, state, gn)


def _layer_weights(l, w_in, b_forget, w_gla_gate_up, b_gla_gate):
    w = w_in[l]
    o = 0
    parts = {}
    for name, width in (("fq", FOX_W), ("fk", FOX_W), ("fv", FOX_W), ("ff", FOX_HEADS), ("gq", GLA_KW), ("gk", GLA_KW),
                        ("gv", GLA_VW), ("gr", GLA_VW), ("ga", GLA_RANK), ("ma", D_MODEL), ("mb", D_MODEL)):
        parts[name] = w[:, o:o + width]
        o += width
    bf = lambda a: a.astype(BF16)
    pad_cols = lambda a: jnp.pad(a, ((0, 0), (0, LANES - a.shape[1])))
    fox = (bf(parts["fq"] * FOX_DH ** -0.5), bf(parts["fk"]), bf(parts["fv"]), bf(pad_cols(parts["ff"])),
           pad_cols(b_forget[l][None, :].astype(F32)))
    wu = jnp.pad(w_gla_gate_up[l], ((0, LANES - GLA_RANK), (0, 0)))
    gla = (bf(parts["gq"]), bf(parts["gk"]), bf(parts["gv"]), bf(parts["gr"]), bf(pad_cols(parts["ga"])), bf(wu),
           b_gla_gate[l][None, :].astype(F32), bf(parts["ma"]), bf(parts["mb"]))
    return fox, gla


def kernel(x_prompt, x_sample, cache_k, cache_v, cache_logf, state_gla, state_conv, page_table, meta_tokens, g_pre_mix, w_in, b_forget, w_gla_gate_up, b_gla_gate, gla_norm_g, w_fox_up, w_gla_up, w_out, g_post_mix, g_pre_ffn, w_up, conv_w, conv_b, w_down, g_post_ffn):
    bsz, seq0, _ = x_prompt.shape
    db, ds, _ = x_sample.shape
    depth = w_in.shape[0]
    assert depth == 1 and ds == 1
    seq = N_META + seq0
    tiles_per_seq = 3
    tm_seq = seq // tiles_per_seq
    assert tm_seq * tiles_per_seq == seq and tm_seq % 16 == 0
    n_p = bsz * seq
    tm_row = 512
    assert n_p % tm_row == 0
    n_pool = cache_k.shape[1]
    l = 0
    row1 = lambda a: a[l][None, :].astype(F32)

    fox_w, gla_w = _layer_weights(l, w_in, b_forget, w_gla_gate_up, b_gla_gate)
    g_pre, gn = row1(g_pre_mix), row1(gla_norm_g)
    wa, wb, wo = w_fox_up[l].astype(BF16), w_gla_up[l].astype(BF16), w_out[l].astype(BF16)
    g_post, g_ffn_pre, g_ffn_post = row1(g_post_mix), row1(g_pre_ffn), row1(g_post_ffn)
    wup, wdn = w_up[l].astype(BF16), w_down[l].astype(BF16)
    cw, cb = conv_w[l].astype(F32), row1(conv_b)

    meta = jnp.broadcast_to(meta_tokens.astype(x_prompt.dtype), (bsz, N_META, D_MODEL))
    xp = jnp.concatenate([meta, x_prompt], axis=1).reshape(n_p, D_MODEL)
    fq, fk, fv, lf, cq, ck = _fox_proj(xp, g_pre, *fox_w, tm=tm_seq, tiles_per_seq=tiles_per_seq, q_dtype=BF16)
    gq, gk, gv, gl, sr, ma, mb = _gla_proj(xp, g_pre, *gla_w, tm=tm_row, mid_dtype=BF16)
    s3 = lambda a: a.reshape(bsz, seq, a.shape[-1])
    fo, kt, vt = _fox_prompt(s3(fq), s3(fk), s3(fv), s3(cq), s3(ck))
    token_major = lambda a: jnp.transpose(a, (0, 3, 1, 2))[None]
    go, gla_p = _gla_prompt(s3(gq), s3(gk), s3(gl), s3(gv), gn)
    x1, h2 = _merge(xp, fo.reshape(n_p, FOX_W), go.reshape(n_p, GLA_VW), sr, ma, mb, wa, wb, wo, g_post, g_ffn_pre,
                    tm=tm_row)
    yp, conv_p = _ffn(h2, x1, wup, cw, cb, wdn, g_ffn_post, tm=tm_seq, tiles_per_seq=tiles_per_seq)
    y_prompt = yp.reshape(bsz, seq, D_MODEL)[:, N_META:]

    xs = x_sample.reshape(db, D_MODEL)
    sq, sk, sv, slf = _fox_proj(xs, g_pre, *fox_w, tm=db, tiles_per_seq=0, q_dtype=F32)
    tq, tk, tv, tl, tsr, tma, tmb = _gla_proj(xs, g_pre, *gla_w, tm=db, mid_dtype=F32)
    v3 = lambda a: a.reshape(db, 1, a.shape[-1])
    so = _fox_sample(v3(sq), v3(sk), v3(sv), v3(slf),
                     jnp.transpose(cache_k[l], (0, 2, 3, 1)), jnp.transpose(cache_v[l], (0, 2, 3, 1)),
                     jnp.transpose(cache_logf[l], (0, 2, 1)), page_table)
    sgo, gla_s = _gla_sample(v3(tq), v3(tk), v3(tl), v3(tv), state_gla[l], gn)
    sx1, sh2 = _merge(xs, so.reshape(db, FOX_W), sgo.reshape(db, GLA_VW), tsr, tma, tmb, wa, wb, wo, g_post,
                      g_ffn_pre, tm=db)
    ys, u_new = _ffn(sh2, sx1, wup, cw, cb, wdn, g_ffn_post, tm=db, tiles_per_seq=0,
                     prev=(state_conv[l][:, 0, :], state_conv[l][:, 1, :]))
    conv_s = jnp.stack([state_conv[l][:, 1, :], u_new], axis=1)

    heads = lambda a, n: a.reshape(1, n, -1, FOX_HEADS, FOX_DH)
    return (y_prompt, ys.reshape(db, 1, D_MODEL),
            token_major(kt), token_major(vt), lf.reshape(1, bsz, seq, FOX_HEADS), gla_p[None], conv_p[None],
            heads(sk, db), heads(sv, db), slf.reshape(1, db, 1, FOX_HEADS), gla_s[None], conv_s[None])
```

```python
import functools

import jax
import jax.numpy as jnp
from jax import lax
from jax.experimental import pallas as pl
from jax.experimental.pallas import tpu as pltpu

D_MODEL = 1024
N_META = 16
FOX_HEADS = 8
FOX_DH = 64
FOX_W = FOX_HEADS * FOX_DH
GLA_HEADS = 4
GLA_DK = 128
GLA_DV = 256
GLA_KW = GLA_HEADS * GLA_DK
GLA_VW = GLA_HEADS * GLA_DV
GLA_RANK = 16
GLA_TAU = 16.0
GLA_CHUNK = 64
D_FF = 2816
CONV_W = 3
EPS = 1e-6
PAGE_SIZE = 128

LANES = 128
ATT_BLOCK = 256
ATT_ROWS = 1024
GLA_BLOCK = 256
FFN_COLS = 256
PAGES_PER_STEP = 32
GLA_SAMPLES_PER_STEP = 4
NEG = -1e30
VMEM_LIMIT = 56 * 1024 * 1024

F32 = jnp.float32
BF16 = jnp.bfloat16

_NT = (((1,), (1,)), ((), ()))


def _dot(a, b):
    return jnp.dot(a, b, preferred_element_type=F32)


def _dot_nt(a, b):
    return lax.dot_general(a, b, _NT, preferred_element_type=F32)


def _rms(x, g):
    return x * lax.rsqrt(jnp.mean(x * x, axis=-1, keepdims=True) + EPS) * g


def _log_sigmoid(x):
    return jnp.minimum(x, 0.0) - jnp.log1p(jnp.exp(-jnp.abs(x)))


def _sigmoid(x):
    return 1.0 / (1.0 + jnp.exp(-x))


def _split3(x):
    x1 = x.astype(BF16)
    r1 = x - x1.astype(F32)
    x2 = r1.astype(BF16)
    x3 = (r1 - x2.astype(F32)).astype(BF16)
    return x1, x2, x3


def _split3_cat(x):
    return jnp.concatenate(_split3(x), axis=1)


def _sum3(y, w):
    return y[:, :w] + y[:, w:2 * w] + y[:, 2 * w:]


def _const_spec(shape):
    return pl.BlockSpec(shape, lambda *_: (0,) * len(shape), pipeline_mode=pl.Buffered(1))


def _params(*sem):
    return pltpu.CompilerParams(dimension_semantics=sem, vmem_limit_bytes=VMEM_LIMIT)


def _fox_proj_kernel(x_ref, g_ref, wq_ref, wk_ref, wv_ref, wf_ref, bf_ref, *rest, tiles_per_seq):
    if tiles_per_seq:
        tri_ref, place_ref, ones_ref, fq_ref, k_ref, v_ref, lf_ref, cq_ref, ck_ref, carry_ref = rest
    else:
        fq_ref, k_ref, v_ref, lf_ref = rest
    h = _rms(x_ref[...], g_ref[...]).astype(BF16)
    fq_ref[...] = _dot(h, wq_ref[...]).astype(fq_ref.dtype)
    k_ref[...] = _dot(h, wk_ref[...])
    v_ref[...] = _dot(h, wv_ref[...])
    lane = lax.broadcasted_iota(jnp.int32, (1, LANES), 1)
    lf = jnp.where(lane < FOX_HEADS, _log_sigmoid(_dot(h, wf_ref[...]) + bf_ref[...]), 0.0)
    lf_ref[...] = lf[:, :FOX_HEADS]
    if tiles_per_seq:
        @pl.when(pl.program_id(0) % tiles_per_seq == 0)
        def _():
            carry_ref[...] = jnp.zeros_like(carry_ref)

        def pack3(x):
            x1, x2, x3 = (t.astype(F32) for t in _split3(x))
            return (x1 + pltpu.roll(x2, FOX_HEADS, 1) + pltpu.roll(x3, 2 * FOX_HEADS, 1)).astype(BF16)

        tm = lf.shape[0]
        cs = _dot(tri_ref[...], pack3(lf))
        cs = cs + pltpu.roll(cs, LANES - FOX_HEADS, 1) + pltpu.roll(cs, LANES - 2 * FOX_HEADS, 1)
        c = carry_ref[...] + jnp.where(lane < FOX_HEADS, cs, 0.0)
        carry_ref[...] = c[tm - 1:tm, :]
        aug = _dot(pack3(c), place_ref[...]) + ones_ref[...]
        cq_ref[...] = aug[:, :FOX_W].astype(BF16)
        ck_ref[...] = aug[:, FOX_W:].astype(BF16)


def _placement():
    import numpy as np
    place = np.zeros((LANES, 2 * FOX_W), np.float32)
    ones = np.zeros((1, 2 * FOX_W), np.float32)
    for h in range(FOX_HEADS):
        base = LANES * (h // 2) + (FOX_DH if h % 2 == 0 else 0)
        for t in range(3):
            place[FOX_HEADS * t + h, base + t] = 1.0
            ones[0, base + 3 + t] = 1.0
            ones[0, FOX_W + base + t] = 1.0
            place[FOX_HEADS * t + h, FOX_W + base + 3 + t] = -1.0
    return jnp.asarray(place, BF16), jnp.asarray(ones, F32)


def _fox_proj(x, g, wq, wk, wv, wf, bf, *, tm, tiles_per_seq, q_dtype):
    n = x.shape[0]
    row = lambda w: pl.BlockSpec((tm, w), lambda i: (i, 0))
    in_specs = [row(D_MODEL), _const_spec((1, D_MODEL)), _const_spec((D_MODEL, FOX_W)),
                _const_spec((D_MODEL, FOX_W)), _const_spec((D_MODEL, FOX_W)),
                _const_spec((D_MODEL, LANES)), _const_spec((1, LANES))]
    out_shape = [jax.ShapeDtypeStruct((n, FOX_W), q_dtype), jax.ShapeDtypeStruct((n, FOX_W), F32),
                 jax.ShapeDtypeStruct((n, FOX_W), F32), jax.ShapeDtypeStruct((n, FOX_HEADS), F32)]
    out_specs = [row(FOX_W), row(FOX_W), row(FOX_W), row(FOX_HEADS)]
    args = [x, g, wq, wk, wv, wf, bf]
    scratch = []
    if tiles_per_seq:
        place, ones = _placement()
        tri = jnp.tril(jnp.ones((tm, tm), BF16))
        args += [tri, place, ones]
        in_specs += [_const_spec((tm, tm)), _const_spec((LANES, 2 * FOX_W)), _const_spec((1, 2 * FOX_W))]
        out_shape += [jax.ShapeDtypeStruct((n, FOX_W), BF16)] * 2
        out_specs += [row(FOX_W), row(FOX_W)]
        scratch = [pltpu.VMEM((1, LANES), F32)]
    return pl.pallas_call(
        functools.partial(_fox_proj_kernel, tiles_per_seq=tiles_per_seq),
        grid=(n // tm,), in_specs=in_specs, out_specs=out_specs, out_shape=out_shape,
        scratch_shapes=scratch, compiler_params=_params("arbitrary"), name="fox_proj")(*args)


def _gla_proj_kernel(x_ref, g_ref, wq_ref, wk_ref, wv_ref, wr_ref, wa_ref, wu_ref, bu_ref, wma_ref, wmb_ref,
                     gq_ref, gk_ref, gv_ref, gl_ref, sr_ref, ma_ref, mb_ref):
    h = _rms(x_ref[...], g_ref[...]).astype(BF16)
    gq_ref[...] = _dot(h, wq_ref[...]).astype(gq_ref.dtype)
    gk_ref[...] = _dot(h, wk_ref[...]).astype(gk_ref.dtype)
    gv_ref[...] = _dot(h, wv_ref[...]).astype(gv_ref.dtype)
    low = _dot(h, wa_ref[...]).astype(BF16)
    gl_ref[...] = _log_sigmoid(_dot(low, wu_ref[...]) + bu_ref[...]) * (1.0 / GLA_TAU)
    r = _dot(h, wr_ref[...])
    sr_ref[...] = (r * _sigmoid(r)).astype(sr_ref.dtype)
    ma_ref[...] = _sigmoid(_dot(h, wma_ref[...])).astype(ma_ref.dtype)
    mb_ref[...] = _sigmoid(_dot(h, wmb_ref[...])).astype(mb_ref.dtype)


def _gla_proj(x, g, wq, wk, wv, wr, wa, wu, bu, wma, wmb, *, tm, mid_dtype):
    n = x.shape[0]
    row = lambda w: pl.BlockSpec((tm, w), lambda i: (i, 0))
    widths = [GLA_KW, GLA_KW, GLA_VW, GLA_KW, GLA_VW, D_MODEL, D_MODEL]
    dtypes = [mid_dtype, mid_dtype, mid_dtype, F32, mid_dtype, mid_dtype, mid_dtype]
    return pl.pallas_call(
        _gla_proj_kernel, grid=(n // tm,),
        in_specs=[row(D_MODEL), _const_spec((1, D_MODEL)), _const_spec((D_MODEL, GLA_KW)),
                  _const_spec((D_MODEL, GLA_KW)), _const_spec((D_MODEL, GLA_VW)), _const_spec((D_MODEL, GLA_VW)),
                  _const_spec((D_MODEL, LANES)), _const_spec((LANES, GLA_KW)), _const_spec((1, GLA_KW)),
                  _const_spec((D_MODEL, D_MODEL)), _const_spec((D_MODEL, D_MODEL))],
        out_specs=[row(w) for w in widths],
        out_shape=[jax.ShapeDtypeStruct((n, w), dt) for w, dt in zip(widths, dtypes)],
        compiler_params=_params("parallel"), name="gla_proj")(x, g, wq, wk, wv, wr, wa, wu, bu, wma, wmb)


def _fox_prompt_kernel(fq_ref, k_ref, v_ref, cq_ref, ck_ref, o_ref, kt_ref, vt_ref, q_s, k_s, v_s, m_s, l_s, acc_s,
                       *, seq):
    n_full = (seq - N_META) // ATT_BLOCK
    full = n_full * ATT_BLOCK
    lane = lax.broadcasted_iota(jnp.int32, (1, LANES), 1)
    lo = lane < FOX_DH
    qp, cq = fq_ref[0], cq_ref[0]
    q_s[0] = jnp.where(lo, qp, cq)
    q_s[1] = jnp.where(lo, cq, qp)
    kp, ck = k_ref[0].astype(BF16), ck_ref[0]
    k_s[0] = jnp.where(lo, kp, ck)
    k_s[1] = jnp.where(lo, ck, kp)
    v_s[...] = v_ref[0].astype(BF16)

    for src, dst in ((k_ref, kt_ref), (v_ref, vt_ref)):
        dst[0, :, :, :full] = src[0, :full, :].T.reshape(2, FOX_DH, full)
        last_t = src[0, seq - LANES:seq, :].T
        dst[0, :, :, full:seq] = last_t[:, LANES - N_META:].reshape(2, FOX_DH, N_META)

    r_io = lax.broadcasted_iota(jnp.int32, (ATT_BLOCK, ATT_BLOCK), 0)
    c_io = lax.broadcasted_iota(jnp.int32, (ATT_BLOCK, ATT_BLOCK), 1)
    diag_mask = c_io <= r_io
    wide = lambda x: jnp.concatenate([x] * (ATT_BLOCK // LANES), axis=1)

    for j in range(n_full):
        k0 = j * ATT_BLOCK
        keys = slice(k0, k0 + ATT_BLOCK)
        for r0 in range(k0, full, ATT_ROWS):
            rows = slice(r0, min(r0 + ATT_ROWS, full))
            for e in range(2):
                s = _dot_nt(q_s[e, rows, :], k_s[e, keys, :])
                if r0 == k0:
                    top = jnp.where(diag_mask, s[:ATT_BLOCK], NEG)
                    s = top if s.shape[0] == ATT_BLOCK else jnp.concatenate([top, s[ATT_BLOCK:]], axis=0)
                m_cur = jnp.max(s, axis=1, keepdims=True)
                if j == 0:
                    m_next = jnp.broadcast_to(m_cur, (s.shape[0], LANES))
                    p = jnp.exp(s - wide(m_next))
                    l_s[e, rows] = jnp.broadcast_to(jnp.sum(p, axis=1, keepdims=True), m_next.shape)
                    acc_s[e, rows] = _dot(p.astype(BF16), v_s[keys, :])
                else:
                    m_prev = m_s[e, rows]
                    m_next = jnp.maximum(m_prev, m_cur)
                    p = jnp.exp(s - wide(m_next))
                    alpha = jnp.exp(m_prev - m_next)
                    l_s[e, rows] = alpha * l_s[e, rows] + jnp.sum(p, axis=1, keepdims=True)
                    acc_s[e, rows] = alpha * acc_s[e, rows] + _dot(p.astype(BF16), v_s[keys, :])
                m_s[e, rows] = m_next
    o_ref[0, :full, :] = jnp.where(lo, acc_s[0] / l_s[0], acc_s[1] / l_s[1]).astype(o_ref.dtype)

    tr = lax.broadcasted_iota(jnp.int32, (N_META, ATT_BLOCK), 0)
    tc = lax.broadcasted_iota(jnp.int32, (N_META, ATT_BLOCK), 1)
    first_new = ATT_BLOCK - N_META
    tail_mask = (tc >= first_new) & (tc - first_new <= tr)
    last = slice(seq - ATT_BLOCK, seq)
    outs = []
    for e in range(2):
        q = q_s[e, full:seq, :]
        s_old = _dot_nt(q, k_s[e, :full, :])
        s_new = jnp.where(tail_mask, _dot_nt(q, k_s[e, last, :]), NEG)
        m = jnp.maximum(jnp.max(s_old, axis=1, keepdims=True), jnp.max(s_new, axis=1, keepdims=True))
        p_old, p_new = jnp.exp(s_old - m), jnp.exp(s_new - m)
        l = jnp.sum(p_old, axis=1, keepdims=True) + jnp.sum(p_new, axis=1, keepdims=True)
        outs.append((_dot(p_old.astype(BF16), v_s[:full, :]) + _dot(p_new.astype(BF16), v_s[last, :])) / l)
    o_ref[0, full:seq, :] = jnp.where(lo, outs[0], outs[1]).astype(o_ref.dtype)


def _fox_prompt(fq, k, v, cq, ck):
    b, seq, _ = fq.shape
    assert (seq - N_META) % ATT_BLOCK == 0 and seq >= ATT_BLOCK
    full = seq - N_META
    spec = pl.BlockSpec((1, seq, LANES), lambda i, j: (i, 0, j))
    t_spec = pl.BlockSpec((1, 2, FOX_DH, seq), lambda i, j: (i, j, 0, 0))
    t_shape = jax.ShapeDtypeStruct((b, FOX_HEADS, FOX_DH, seq), F32)
    return pl.pallas_call(
        functools.partial(_fox_prompt_kernel, seq=seq), grid=(b, FOX_W // LANES),
        in_specs=[spec] * 5, out_specs=[spec, t_spec, t_spec],
        out_shape=[jax.ShapeDtypeStruct((b, seq, FOX_W), BF16), t_shape, t_shape],
        scratch_shapes=[pltpu.VMEM((2, seq, LANES), BF16), pltpu.VMEM((2, seq, LANES), BF16),
                        pltpu.VMEM((seq, LANES), BF16), pltpu.VMEM((2, full, LANES), F32),
                        pltpu.VMEM((2, full, LANES), F32), pltpu.VMEM((2, full, LANES), F32)],
        compiler_params=_params("parallel", "parallel"), name="fox_prompt")(fq, k, v, cq, ck)


def _gla_prompt_kernel(q_ref, k_ref, g_ref, v_ref, gn_ref, tri_ref, ones_ref, o_ref, st_ref, s_s, *, seq):
    n_blocks = (seq - N_META) // GLA_BLOCK
    scale = GLA_DK ** -0.5
    gn = gn_ref[...]
    s_s[...] = jnp.zeros_like(s_s)

    def block(h, q, k, g, v, rows):
        r_io = lax.broadcasted_iota(jnp.int32, (rows, rows), 0)
        c_io = lax.broadcasted_iota(jnp.int32, (rows, rows), 1)
        intra = (r_io // GLA_CHUNK == c_io // GLA_CHUNK) & (c_io <= r_io)
        col_chunk = lax.broadcasted_iota(jnp.int32, (GLA_DK, rows), 1) // GLA_CHUNK
        tri = tri_ref[:rows, :rows]
        ones = ones_ref[:rows, :rows]
        g3 = _split3_cat(g)
        b = _sum3(_dot(tri, g3), GLA_DK)
        b_last = _sum3(_dot(ones, g3), GLA_DK)
        kf = k.astype(F32)
        qg = (q.astype(F32) * (scale * jnp.exp(b))).astype(BF16)
        kb = (kf * jnp.exp(-b)).astype(BF16)
        kd_t = (kf * jnp.exp(b_last - b)).T
        decay_t = jnp.exp(b_last).T
        a = jnp.where(intra, _dot_nt(qg, kb), 0.0).astype(BF16)
        o = _dot(a, v)
        outs = []
        for c in range(rows // GLA_CHUNK):
            sl = slice(c * GLA_CHUNK, (c + 1) * GLA_CHUNK)
            s = s_s[h]
            oc = o[sl] + _dot(qg[sl], s.astype(BF16))
            upd = _dot(jnp.where(col_chunk == c, kd_t, 0.0).astype(BF16), v)
            s_s[h] = s * decay_t[:, c * GLA_CHUNK:c * GLA_CHUNK + 1] + upd
            outs.append(_rms(oc, gn))
        return outs

    ks = lambda h: slice(h * GLA_DK, (h + 1) * GLA_DK)
    vs = lambda h: slice(h * GLA_DV, (h + 1) * GLA_DV)

    lead = 2 * GLA_CHUNK - N_META
    pad = lambda ref, cols: jnp.concatenate(
        [jnp.zeros((lead, cols.stop - cols.start), ref.dtype), ref[0, :N_META, cols]], axis=0)
    for h in range(GLA_HEADS):
        outs = block(h, pad(q_ref, ks(h)), pad(k_ref, ks(h)), pad(g_ref, ks(h)), pad(v_ref, vs(h)), 2 * GLA_CHUNK)
        o_ref[0, :N_META, vs(h)] = outs[1][GLA_CHUNK - N_META:].astype(o_ref.dtype)

    def body(n, carry):
        r0 = pl.multiple_of(N_META + n * GLA_BLOCK, N_META)
        rows = pl.ds(r0, GLA_BLOCK)
        for h in range(GLA_HEADS):
            outs = block(h, q_ref[0, rows, ks(h)], k_ref[0, rows, ks(h)], g_ref[0, rows, ks(h)],
                         v_ref[0, rows, vs(h)], GLA_BLOCK)
            o_ref[0, rows, vs(h)] = jnp.concatenate(outs, axis=0).astype(o_ref.dtype)
        return carry

    lax.fori_loop(0, n_blocks, body, 0)
    st_ref[0] = s_s[...]


def _gla_prompt(gq, gk, gl, gv, gn):
    b, seq, _ = gq.shape
    assert (seq - N_META) % GLA_BLOCK == 0
    idx = lax.broadcasted_iota(jnp.int32, (GLA_BLOCK, GLA_BLOCK), 0), lax.broadcasted_iota(jnp.int32, (GLA_BLOCK, GLA_BLOCK), 1)
    same = idx[0] // GLA_CHUNK == idx[1] // GLA_CHUNK
    tri = (same & (idx[1] <= idx[0])).astype(BF16)
    ones = same.astype(BF16)
    kspec = pl.BlockSpec((1, seq, GLA_KW), lambda i: (i, 0, 0))
    vspec = pl.BlockSpec((1, seq, GLA_VW), lambda i: (i, 0, 0))
    state = (GLA_HEADS, GLA_DK, GLA_DV)
    return pl.pallas_call(
        functools.partial(_gla_prompt_kernel, seq=seq), grid=(b,),
        in_specs=[kspec, kspec, kspec, vspec, _const_spec((1, GLA_DV)),
                  _const_spec((GLA_BLOCK, GLA_BLOCK)), _const_spec((GLA_BLOCK, GLA_BLOCK))],
        out_specs=[vspec, pl.BlockSpec((1,) + state, lambda i: (i, 0, 0, 0))],
        out_shape=[jax.ShapeDtypeStruct((b, seq, GLA_VW), BF16), jax.ShapeDtypeStruct((b,) + state, F32)],
        scratch_shapes=[pltpu.VMEM(state, F32)],
        compiler_params=_params("parallel"), name="gla_prompt")(gq, gk, gl, gv, gn, tri, ones)


def _merge_kernel(x_ref, fo_ref, go_ref, sr_ref, ma_ref, mb_ref, wa_ref, wb_ref, wo_ref, gpost_ref, gpre_ref,
                  x1_ref, h2_ref):
    gated = (go_ref[...].astype(F32) * sr_ref[...].astype(F32)).astype(BF16)
    y = (ma_ref[...].astype(F32) * _dot(fo_ref[...].astype(BF16), wa_ref[...])
         + mb_ref[...].astype(F32) * _dot(gated, wb_ref[...]))
    mix = _dot(y.astype(BF16), wo_ref[...])
    x1 = x_ref[...] + _rms(mix, gpost_ref[...])
    x1_ref[...] = x1
    h2_ref[...] = _rms(x1, gpre_ref[...]).astype(BF16)


def _merge(x, fo, go, sr, ma, mb, wa, wb, wo, gpost, gpre, *, tm):
    n = x.shape[0]
    row = lambda w: pl.BlockSpec((tm, w), lambda i: (i, 0))
    return pl.pallas_call(
        _merge_kernel, grid=(n // tm,),
        in_specs=[row(D_MODEL), row(FOX_W), row(GLA_VW), row(GLA_VW), row(D_MODEL), row(D_MODEL),
                  _const_spec((FOX_W, D_MODEL)), _const_spec((GLA_VW, D_MODEL)), _const_spec((D_MODEL, D_MODEL)),
                  _const_spec((1, D_MODEL)), _const_spec((1, D_MODEL))],
        out_specs=[row(D_MODEL), row(D_MODEL)],
        out_shape=[jax.ShapeDtypeStruct((n, D_MODEL), F32), jax.ShapeDtypeStruct((n, D_MODEL), BF16)],
        compiler_params=_params("parallel"), name="merge")(x, fo, go, sr, ma, mb, wa, wb, wo, gpost, gpre)


def _gelu(x):
    return 0.5 * x * (1.0 + jnp.tanh(0.7978845608028654 * (x + 0.044715 * x * x * x)))


def _ffn_kernel(h_ref, x_ref, wup_ref, cw_ref, cb_ref, wdn_ref, g_ref, *rest, tiles_per_seq):
    if tiles_per_seq:
        y_ref, conv_ref, act_s, u_s, carry_s = rest
    else:
        p2_ref, p1_ref, y_ref, unew_ref, act_s = rest
    h = h_ref[...]
    tm = h.shape[0]
    if tiles_per_seq:
        @pl.when(pl.program_id(0) % tiles_per_seq == 0)
        def _():
            carry_s[...] = jnp.zeros_like(carry_s)

    for c in range(D_FF // FFN_COLS):
        halves = []
        for col in (c * FFN_COLS, D_FF + c * FFN_COLS):
            cols = slice(col, col + FFN_COLS)
            u = _dot(h, wup_ref[:, cols])
            if tiles_per_seq:
                u_s[0:8, :] = carry_s[:, cols]
                u_s[8:8 + tm, :] = u
                prev2, prev1 = u_s[6:6 + tm, :], u_s[7:7 + tm, :]
                carry_s[:, cols] = u_s[tm:tm + 8, :]
                conv_ref[0, :, cols] = u_s[tm + 6:tm + 8, :]
            else:
                prev2, prev1 = p2_ref[:, cols], p1_ref[:, cols]
                unew_ref[:, cols] = u
            halves.append(cb_ref[:, cols] + prev2 * cw_ref[0:1, cols] + prev1 * cw_ref[1:2, cols]
                          + u * cw_ref[2:3, cols])
        val, gate = halves
        act_s[:, c * FFN_COLS:(c + 1) * FFN_COLS] = (_gelu(gate) * val).astype(BF16)
    y_ref[...] = x_ref[...] + _rms(_dot(act_s[...], wdn_ref[...]), g_ref[...])


def _ffn(h, x, wup, cw, cb, wdn, g, *, tm, tiles_per_seq, prev=None):
    n = h.shape[0]
    row = lambda w: pl.BlockSpec((tm, w), lambda i: (i, 0))
    in_specs = [row(D_MODEL), row(D_MODEL), _const_spec((D_MODEL, 2 * D_FF)), _const_spec((CONV_W, 2 * D_FF)),
                _const_spec((1, 2 * D_FF)), _const_spec((D_FF, D_MODEL)), _const_spec((1, D_MODEL))]
    args = [h, x, wup, cw, cb, wdn, g]
    if tiles_per_seq:
        n_seq = n // (tm * tiles_per_seq)
        out_shape = [jax.ShapeDtypeStruct((n, D_MODEL), F32), jax.ShapeDtypeStruct((n_seq, CONV_W - 1, 2 * D_FF), F32)]
        out_specs = [row(D_MODEL), pl.BlockSpec((1, CONV_W - 1, 2 * D_FF), lambda i: (i // tiles_per_seq, 0, 0))]
        scratch = [pltpu.VMEM((tm, D_FF), BF16), pltpu.VMEM((tm + 8, FFN_COLS), F32), pltpu.VMEM((8, 2 * D_FF), F32)]
    else:
        args += list(prev)
        in_specs += [row(2 * D_FF), row(2 * D_FF)]
        out_shape = [jax.ShapeDtypeStruct((n, D_MODEL), F32), jax.ShapeDtypeStruct((n, 2 * D_FF), F32)]
        out_specs = [row(D_MODEL), row(2 * D_FF)]
        scratch = [pltpu.VMEM((tm, D_FF), BF16)]
    return pl.pallas_call(
        functools.partial(_ffn_kernel, tiles_per_seq=tiles_per_seq), grid=(n // tm,),
        in_specs=in_specs, out_specs=out_specs, out_shape=out_shape, scratch_shapes=scratch,
        compiler_params=_params("arbitrary"), name="conv_ffn")(*args)


def _fox_sample_kernel(pt_ref, q_ref, kn_ref, vn_ref, lfn_ref, uo_ref, w_ref, *rest):
    del pt_ref
    npg = PAGES_PER_STEP
    k_refs, v_refs, l_refs = rest[:npg], rest[npg:2 * npg], rest[2 * npg:3 * npg]
    o_ref, qc_s, m_s, l_s, acc_s, carry_s, cn_s, row_s = rest[3 * npg:]
    g = pl.program_id(1)
    eye = lax.broadcasted_iota(jnp.int32, (LANES, LANES), 0) == lax.broadcasted_iota(jnp.int32, (LANES, LANES), 1)
    to_col = lambda r: jnp.sum(jnp.where(eye, r, 0.0), axis=1, keepdims=True)
    to_row = lambda c: jnp.sum(jnp.where(eye, c, 0.0), axis=0, keepdims=True)
    sub = lax.broadcasted_iota(jnp.int32, (FOX_HEADS, LANES), 0)
    lane0 = lax.broadcasted_iota(jnp.int32, (FOX_DH, LANES), 1) == 0

    @pl.when(g == 0)
    def _():
        row_s[...] = jnp.zeros_like(row_s)
        row_s[0:1, 0:FOX_HEADS] = lfn_ref[0]
        cn_s[...] = jnp.broadcast_to(to_col(row_s[0:1, :])[:FOX_HEADS], (FOX_HEADS, LANES))
        carry_s[...] = jnp.zeros_like(carry_s)
        s_new = jnp.zeros((FOX_HEADS, LANES), F32)
        for j in range(FOX_HEADS // 2):
            lanes = slice(j * LANES, (j + 1) * LANES)
            qcol = to_col(q_ref[0, :, lanes])
            qc_s[lanes, :] = jnp.broadcast_to(qcol, (LANES, LANES))
            qk = qcol * to_col(kn_ref[0, :, lanes])
            vcol = to_col(vn_ref[0, :, lanes])
            for e in range(2):
                h = 2 * j + e
                rows = slice(e * FOX_DH, (e + 1) * FOX_DH)
                s_new = jnp.where(sub == h, jnp.sum(qk[rows], axis=0, keepdims=True), s_new)
                acc_s[h] = jnp.where(lane0, vcol[rows], 0.0)
        m_s[...] = s_new
        l_s[...] = jnp.ones_like(l_s)

    lf = jnp.concatenate([l_refs[i][0] for i in range(npg)], axis=0)
    both = _dot(_split3_cat(lf), uo_ref[...])
    in_page, page_tot = both[:, :LANES], both[:, LANES:]
    later = _sum3(_dot(w_ref[...], _split3_cat(page_tot)), LANES)
    bias = in_page + later + jnp.concatenate([carry_s[...] + cn_s[...]] * npg, axis=0)
    carry_s[...] = carry_s[...] + later[:FOX_HEADS] + page_tot[:FOX_HEADS]

    pages = [jnp.zeros((FOX_HEADS, LANES), F32)] * npg
    for h in range(FOX_HEADS):
        qc = qc_s[h * FOX_DH:(h + 1) * FOX_DH, :]
        for i in range(npg):
            pages[i] = jnp.where(sub == h, jnp.sum(qc * k_refs[i][0, h], axis=0, keepdims=True), pages[i])
    s = jnp.concatenate(pages, axis=0) + bias
    per_head = lambda x: x.reshape(npg, FOX_HEADS, LANES)
    m_prev = m_s[...]
    m_next = jnp.maximum(m_prev, jnp.max(jnp.max(per_head(s), axis=0), axis=1, keepdims=True))
    p = jnp.exp(s - jnp.concatenate([m_next] * npg, axis=0))
    alpha = jnp.exp(m_prev - m_next)
    l_s[...] = alpha * l_s[...] + jnp.sum(jnp.sum(per_head(p), axis=0), axis=1, keepdims=True)
    m_s[...] = m_next
    for h in range(FOX_HEADS):
        acc = acc_s[h] * alpha[h:h + 1, :]
        for i in range(npg):
            r = i * FOX_HEADS + h
            acc = acc + p[r:r + 1, :] * v_refs[i][0, h]
        acc_s[h] = acc

    @pl.when(g == pl.num_programs(1) - 1)
    def _():
        for j in range(FOX_HEADS // 2):
            cols = [jnp.sum(acc_s[2 * j + e], axis=1, keepdims=True) / l_s[2 * j + e:2 * j + e + 1, 0:1]
                    for e in range(2)]
            o_ref[0, :, j * LANES:(j + 1) * LANES] = to_row(jnp.concatenate(cols, axis=0))


def _fox_sample(q, kn, vn, lfn, cache_kt, cache_vt, cache_lt, page_table):
    db = q.shape[0]
    n_pages = page_table.shape[1]
    npg = PAGES_PER_STEP
    assert n_pages % npg == 0
    groups = n_pages // npg
    tok = jnp.arange(PAGE_SIZE)
    after = (tok[:, None] > tok[None, :]).astype(BF16)
    uo = jnp.concatenate([jnp.concatenate([after, jnp.ones_like(after)], axis=1)] * 3, axis=0)
    r = jnp.arange(npg * FOX_HEADS)
    w = ((r[:, None] % FOX_HEADS == r[None, :] % FOX_HEADS) & (r[None, :] // FOX_HEADS > r[:, None] // FOX_HEADS)).astype(BF16)

    def page_spec(i, shape):
        return pl.BlockSpec((1,) + shape, lambda b, g, pt: (pt[b, (groups - 1 - g) * npg + i],) + (0,) * len(shape))

    vec = lambda width: pl.BlockSpec((1, 1, width), lambda b, g, pt: (b, 0, 0))
    const = lambda shape: pl.BlockSpec(shape, lambda b, g, pt: (0,) * len(shape))
    kv_page = (FOX_HEADS, FOX_DH, PAGE_SIZE)
    grid_spec = pltpu.PrefetchScalarGridSpec(
        num_scalar_prefetch=1, grid=(db, groups),
        in_specs=[vec(FOX_W), vec(FOX_W), vec(FOX_W), vec(FOX_HEADS), const(uo.shape), const(w.shape)]
        + [page_spec(i, kv_page) for i in range(npg)] * 2 + [page_spec(i, (FOX_HEADS, PAGE_SIZE)) for i in range(npg)],
        out_specs=vec(FOX_W),
        scratch_shapes=[pltpu.VMEM((FOX_W, LANES), F32), pltpu.VMEM((FOX_HEADS, LANES), F32),
                        pltpu.VMEM((FOX_HEADS, LANES), F32), pltpu.VMEM((FOX_HEADS, FOX_DH, LANES), F32),
                        pltpu.VMEM((FOX_HEADS, LANES), F32), pltpu.VMEM((FOX_HEADS, LANES), F32),
                        pltpu.VMEM((8, LANES), F32)])
    return pl.pallas_call(
        _fox_sample_kernel, grid_spec=grid_spec, out_shape=jax.ShapeDtypeStruct((db, 1, FOX_W), F32),
        compiler_params=_params("parallel", "arbitrary"), name="fox_sample")(
            page_table, q, kn, vn, lfn, uo, w, *([cache_kt] * npg), *([cache_vt] * npg), *([cache_lt] * npg))


def _gla_sample_kernel(q_ref, k_ref, g_ref, v_ref, s_ref, gn_ref, o_ref, sn_ref):
    eye = lax.broadcasted_iota(jnp.int32, (GLA_DK, GLA_DK), 0) == lax.broadcasted_iota(jnp.int32, (GLA_DK, GLA_DK), 1)
    col = lambda r: jnp.sum(jnp.where(eye, r, 0.0), axis=1, keepdims=True)
    scale = GLA_DK ** -0.5
    for i in range(q_ref.shape[0]):
        for h in range(GLA_HEADS):
            ks = slice(h * GLA_DK, (h + 1) * GLA_DK)
            vs = slice(h * GLA_DV, (h + 1) * GLA_DV)
            s = s_ref[i, h] * col(jnp.exp(g_ref[i, :, ks])) + col(k_ref[i, :, ks]) * v_ref[i, :, vs]
            sn_ref[i, h] = s
            qc = col(q_ref[i, :, ks] * scale).astype(BF16).astype(F32)
            o = jnp.sum(qc * s.astype(BF16).astype(F32), axis=0, keepdims=True)
            o_ref[i, :, vs] = _rms(o, gn_ref[...])


def _gla_sample(gq, gk, gl, gv, state, gn):
    db = gq.shape[0]
    sb = GLA_SAMPLES_PER_STEP
    assert db % sb == 0
    kspec = pl.BlockSpec((sb, 1, GLA_KW), lambda b: (b, 0, 0))
    vspec = pl.BlockSpec((sb, 1, GLA_VW), lambda b: (b, 0, 0))
    sspec = pl.BlockSpec((sb, GLA_HEADS, GLA_DK, GLA_DV), lambda b: (b, 0, 0, 0))
    return pl.pallas_call(
        _gla_sample_kernel, grid=(db // sb,),
        in_specs=[kspec, kspec, kspec, vspec, sspec, pl.BlockSpec((1, GLA_DV), lambda b: (0, 0))],
        out_specs=[vspec, sspec],
        out_shape=[jax.ShapeDtypeStruct((db, 1, GLA_VW), F32), jax.ShapeDtypeStruct(state.shape, F32)],
        compiler_params=_params("parallel"), name="gla_sample")(gq, gk, gl, gv, state, gn)


def _layer_weights(l, w_in, b_forget, w_gla_gate_up, b_gla_gate):
    w = w_in[l]
    o = 0
    parts = {}
    for name, width in (("fq", FOX_W), ("fk", FOX_W), ("fv", FOX_W), ("ff", FOX_HEADS), ("gq", GLA_KW), ("gk", GLA_KW),
                        ("gv", GLA_VW), ("gr", GLA_VW), ("ga", GLA_RANK), ("ma", D_MODEL), ("mb", D_MODEL)):
        parts[name] = w[:, o:o + width]
        o += width
    bf = lambda a: a.astype(BF16)
    pad_cols = lambda a: jnp.pad(a, ((0, 0), (0, LANES - a.shape[1])))
    fox = (bf(parts["fq"] * FOX_DH ** -0.5), bf(parts["fk"]), bf(parts["fv"]), bf(pad_cols(parts["ff"])),
           pad_cols(b_forget[l][None, :].astype(F32)))
    wu = jnp.pad(w_gla_gate_up[l], ((0, LANES - GLA_RANK), (0, 0)))
    gla = (bf(parts["gq"]), bf(parts["gk"]), bf(parts["gv"]), bf(parts["gr"]), bf(pad_cols(parts["ga"])), bf(wu),
           b_gla_gate[l][None, :].astype(F32), bf(parts["ma"]), bf(parts["mb"]))
    return fox, gla


def kernel(x_prompt, x_sample, cache_k, cache_v, cache_logf, state_gla, state_conv, page_table, meta_tokens, g_pre_mix, w_in, b_forget, w_gla_gate_up, b_gla_gate, gla_norm_g, w_fox_up, w_gla_up, w_out, g_post_mix, g_pre_ffn, w_up, conv_w, conv_b, w_down, g_post_ffn):
    bsz, seq0, _ = x_prompt.shape
    db, ds, _ = x_sample.shape
    depth = w_in.shape[0]
    assert depth == 1 and ds == 1
    seq = N_META + seq0
    tiles_per_seq = 3
    tm_seq = seq // tiles_per_seq
    assert tm_seq * tiles_per_seq == seq and tm_seq % 16 == 0
    n_p = bsz * seq
    tm_row = 512
    assert n_p % tm_row == 0
    n_pool = cache_k.shape[1]
    l = 0
    row1 = lambda a: a[l][None, :].astype(F32)

    fox_w, gla_w = _layer_weights(l, w_in, b_forget, w_gla_gate_up, b_gla_gate)
    g_pre, gn = row1(g_pre_mix), row1(gla_norm_g)
    wa, wb, wo = w_fox_up[l].astype(BF16), w_gla_up[l].astype(BF16), w_out[l].astype(BF16)
    g_post, g_ffn_pre, g_ffn_post = row1(g_post_mix), row1(g_pre_ffn), row1(g_post_ffn)
    wup, wdn = w_up[l].astype(BF16), w_down[l].astype(BF16)
    cw, cb = conv_w[l].astype(F32), row1(conv_b)

    meta = jnp.broadcast_to(meta_tokens.astype(x_prompt.dtype), (bsz, N_META, D_MODEL))
    xp = jnp.concatenate([meta, x_prompt], axis=1).reshape(n_p, D_MODEL)
    fq, fk, fv, lf, cq, ck = _fox_proj(xp, g_pre, *fox_w, tm=tm_seq, tiles_per_seq=tiles_per_seq, q_dtype=BF16)
    gq, gk, gv, gl, sr, ma, mb = _gla_proj(xp, g_pre, *gla_w, tm=tm_row, mid_dtype=BF16)
    s3 = lambda a: a.reshape(bsz, seq, a.shape[-1])
    fo, kt, vt = _fox_prompt(s3(fq), s3(fk), s3(fv), s3(cq), s3(ck))
    token_major = lambda a: jnp.transpose(a, (0, 3, 1, 2))[None]
    go, gla_p = _gla_prompt(s3(gq), s3(gk), s3(gl), s3(gv), gn)
    x1, h2 = _merge(xp, fo.reshape(n_p, FOX_W), go.reshape(n_p, GLA_VW), sr, ma, mb, wa, wb, wo, g_post, g_ffn_pre,
                    tm=tm_row)
    yp, conv_p = _ffn(h2, x1, wup, cw, cb, wdn, g_ffn_post, tm=tm_seq, tiles_per_seq=tiles_per_seq)
    y_prompt = yp.reshape(bsz, seq, D_MODEL)[:, N_META:]

    xs = x_sample.reshape(db, D_MODEL)
    sq, sk, sv, slf = _fox_proj(xs, g_pre, *fox_w, tm=db, tiles_per_seq=0, q_dtype=F32)
    tq, tk, tv, tl, tsr, tma, tmb = _gla_proj(xs, g_pre, *gla_w, tm=db, mid_dtype=F32)
    v3 = lambda a: a.reshape(db, 1, a.shape[-1])
    so = _fox_sample(v3(sq), v3(sk), v3(sv), v3(slf),
                     jnp.transpose(cache_k[l], (0, 2, 3, 1)), jnp.transpose(cache_v[l], (0, 2, 3, 1)),
                     jnp.transpose(cache_logf[l], (0, 2, 1)), page_table)
    sgo, gla_s = _gla_sample(v3(tq), v3(tk), v3(tl), v3(tv), state_gla[l], gn)
    sx1, sh2 = _merge(xs, so.reshape(db, FOX_W), sgo.reshape(db, GLA_VW), tsr, tma, tmb, wa, wb, wo, g_post,
                      g_ffn_pre, tm=db)
    ys, u_new = _ffn(sh2, sx1, wup, cw, cb, wdn, g_ffn_post, tm=db, tiles_per_seq=0,
                     prev=(state_conv[l][:, 0, :], state_conv[l][:, 1, :]))
    conv_s = jnp.stack([state_conv[l][:, 1, :], u_new], axis=1)

    heads = lambda a, n: a.reshape(1, n, -1, FOX_HEADS, FOX_DH)
    return (y_prompt, ys.reshape(db, 1, D_MODEL),
            token_major(kt), token_major(vt), lf.reshape(1, bsz, seq, FOX_HEADS), gla_p[None], conv_p[None],
            heads(sk, db), heads(sv, db), slf.reshape(1, db, 1, FOX_HEADS), gla_s[None], conv_s[None])
```

```python
import functools

import jax
import jax.numpy as jnp
from jax import lax
from jax.experimental import pallas as pl
from jax.experimental.pallas import tpu as pltpu

D_MODEL = 1024
N_META = 16
FOX_HEADS = 8
FOX_DH = 64
FOX_W = FOX_HEADS * FOX_DH
GLA_HEADS = 4
GLA_DK = 128
GLA_DV = 256
GLA_KW = GLA_HEADS * GLA_DK
GLA_VW = GLA_HEADS * GLA_DV
GLA_RANK = 16
GLA_TAU = 16.0
GLA_CHUNK = 64
D_FF = 2816
CONV_W = 3
EPS = 1e-6
PAGE_SIZE = 128

LANES = 128
ATT_BLOCK = 256
ATT_ROWS = 1024
GLA_BLOCK = 256
FFN_COLS = 256
FFN_DOWN_GROUP = 6
PAGES_PER_STEP = 32
GLA_SAMPLES_PER_STEP = 4
NEG = -1e30
VMEM_LIMIT = 56 * 1024 * 1024

F32 = jnp.float32
BF16 = jnp.bfloat16

_NT = (((1,), (1,)), ((), ()))


def _dot(a, b):
    return jnp.dot(a, b, preferred_element_type=F32)


def _dot_nt(a, b):
    return lax.dot_general(a, b, _NT, preferred_element_type=F32)


def _rms(x, g):
    return x * lax.rsqrt(jnp.mean(x * x, axis=-1, keepdims=True) + EPS) * g


def _log_sigmoid(x):
    return jnp.minimum(x, 0.0) - jnp.log1p(jnp.exp(-jnp.abs(x)))


def _sigmoid(x):
    return 1.0 / (1.0 + jnp.exp(-x))


def _split3(x):
    x1 = x.astype(BF16)
    r1 = x - x1.astype(F32)
    x2 = r1.astype(BF16)
    x3 = (r1 - x2.astype(F32)).astype(BF16)
    return x1, x2, x3


def _split3_cat(x):
    return jnp.concatenate(_split3(x), axis=1)


def _sum3(y, w):
    return y[:, :w] + y[:, w:2 * w] + y[:, 2 * w:]


def _const_spec(shape):
    return pl.BlockSpec(shape, lambda *_: (0,) * len(shape), pipeline_mode=pl.Buffered(1))


def _params(*sem):
    return pltpu.CompilerParams(dimension_semantics=sem, vmem_limit_bytes=VMEM_LIMIT)


def _fox_proj_kernel(x_ref, g_ref, wq_ref, wk_ref, wv_ref, wf_ref, bf_ref, *rest, tiles_per_seq):
    if tiles_per_seq:
        tri_ref, place_ref, ones_ref, fq_ref, k_ref, v_ref, lf_ref, cq_ref, ck_ref, carry_ref = rest
    else:
        fq_ref, k_ref, v_ref, lf_ref = rest
    h = _rms(x_ref[...], g_ref[...]).astype(BF16)
    fq_ref[...] = _dot(h, wq_ref[...]).astype(fq_ref.dtype)
    k_ref[...] = _dot(h, wk_ref[...])
    v_ref[...] = _dot(h, wv_ref[...])
    lane = lax.broadcasted_iota(jnp.int32, (1, LANES), 1)
    lf = jnp.where(lane < FOX_HEADS, _log_sigmoid(_dot(h, wf_ref[...]) + bf_ref[...]), 0.0)
    lf_ref[...] = lf[:, :FOX_HEADS]
    if tiles_per_seq:
        @pl.when(pl.program_id(0) % tiles_per_seq == 0)
        def _():
            carry_ref[...] = jnp.zeros_like(carry_ref)

        def pack3(x):
            x1, x2, x3 = (t.astype(F32) for t in _split3(x))
            return (x1 + pltpu.roll(x2, FOX_HEADS, 1) + pltpu.roll(x3, 2 * FOX_HEADS, 1)).astype(BF16)

        tm = lf.shape[0]
        cs = _dot(tri_ref[...], pack3(lf))
        cs = cs + pltpu.roll(cs, LANES - FOX_HEADS, 1) + pltpu.roll(cs, LANES - 2 * FOX_HEADS, 1)
        c = carry_ref[...] + jnp.where(lane < FOX_HEADS, cs, 0.0)
        carry_ref[...] = c[tm - 1:tm, :]
        aug = _dot(pack3(c), place_ref[...]) + ones_ref[...]
        cq_ref[...] = aug[:, :FOX_W].astype(BF16)
        ck_ref[...] = aug[:, FOX_W:].astype(BF16)


def _placement():
    import numpy as np
    place = np.zeros((LANES, 2 * FOX_W), np.float32)
    ones = np.zeros((1, 2 * FOX_W), np.float32)
    for h in range(FOX_HEADS):
        base = LANES * (h // 2) + (FOX_DH if h % 2 == 0 else 0)
        for t in range(3):
            place[FOX_HEADS * t + h, base + t] = 1.0
            ones[0, base + 3 + t] = 1.0
            ones[0, FOX_W + base + t] = 1.0
            place[FOX_HEADS * t + h, FOX_W + base + 3 + t] = -1.0
    return jnp.asarray(place, BF16), jnp.asarray(ones, F32)


def _fox_proj(x, g, wq, wk, wv, wf, bf, *, tm, tiles_per_seq, q_dtype):
    n = x.shape[0]
    row = lambda w: pl.BlockSpec((tm, w), lambda i: (i, 0))
    in_specs = [row(D_MODEL), _const_spec((1, D_MODEL)), _const_spec((D_MODEL, FOX_W)),
                _const_spec((D_MODEL, FOX_W)), _const_spec((D_MODEL, FOX_W)),
                _const_spec((D_MODEL, LANES)), _const_spec((1, LANES))]
    out_shape = [jax.ShapeDtypeStruct((n, FOX_W), q_dtype), jax.ShapeDtypeStruct((n, FOX_W), F32),
                 jax.ShapeDtypeStruct((n, FOX_W), F32), jax.ShapeDtypeStruct((n, FOX_HEADS), F32)]
    out_specs = [row(FOX_W), row(FOX_W), row(FOX_W), row(FOX_HEADS)]
    args = [x, g, wq, wk, wv, wf, bf]
    scratch = []
    if tiles_per_seq:
        place, ones = _placement()
        tri = jnp.tril(jnp.ones((tm, tm), BF16))
        args += [tri, place, ones]
        in_specs += [_const_spec((tm, tm)), _const_spec((LANES, 2 * FOX_W)), _const_spec((1, 2 * FOX_W))]
        out_shape += [jax.ShapeDtypeStruct((n, FOX_W), BF16)] * 2
        out_specs += [row(FOX_W), row(FOX_W)]
        scratch = [pltpu.VMEM((1, LANES), F32)]
    return pl.pallas_call(
        functools.partial(_fox_proj_kernel, tiles_per_seq=tiles_per_seq),
        grid=(n // tm,), in_specs=in_specs, out_specs=out_specs, out_shape=out_shape,
        scratch_shapes=scratch, compiler_params=_params("arbitrary"), name="fox_proj")(*args)


def _gla_proj_kernel(x_ref, g_ref, wq_ref, wk_ref, wv_ref, wr_ref, wa_ref, wu_ref, bu_ref, wma_ref, wmb_ref,
                     gq_ref, gk_ref, gv_ref, gl_ref, sr_ref, ma_ref, mb_ref):
    h = _rms(x_ref[...], g_ref[...]).astype(BF16)
    gq_ref[...] = _dot(h, wq_ref[...]).astype(gq_ref.dtype)
    gk_ref[...] = _dot(h, wk_ref[...]).astype(gk_ref.dtype)
    gv_ref[...] = _dot(h, wv_ref[...]).astype(gv_ref.dtype)
    low = _dot(h, wa_ref[...]).astype(BF16)
    gl_ref[...] = _log_sigmoid(_dot(low, wu_ref[...]) + bu_ref[...]) * (1.0 / GLA_TAU)
    r = _dot(h, wr_ref[...])
    sr_ref[...] = (r * _sigmoid(r)).astype(sr_ref.dtype)
    ma_ref[...] = _sigmoid(_dot(h, wma_ref[...])).astype(ma_ref.dtype)
    mb_ref[...] = _sigmoid(_dot(h, wmb_ref[...])).astype(mb_ref.dtype)


def _gla_proj(x, g, wq, wk, wv, wr, wa, wu, bu, wma, wmb, *, tm, mid_dtype):
    n = x.shape[0]
    row = lambda w: pl.BlockSpec((tm, w), lambda i: (i, 0))
    widths = [GLA_KW, GLA_KW, GLA_VW, GLA_KW, GLA_VW, D_MODEL, D_MODEL]
    dtypes = [mid_dtype, mid_dtype, mid_dtype, F32, mid_dtype, mid_dtype, mid_dtype]
    return pl.pallas_call(
        _gla_proj_kernel, grid=(n // tm,),
        in_specs=[row(D_MODEL), _const_spec((1, D_MODEL)), _const_spec((D_MODEL, GLA_KW)),
                  _const_spec((D_MODEL, GLA_KW)), _const_spec((D_MODEL, GLA_VW)), _const_spec((D_MODEL, GLA_VW)),
                  _const_spec((D_MODEL, LANES)), _const_spec((LANES, GLA_KW)), _const_spec((1, GLA_KW)),
                  _const_spec((D_MODEL, D_MODEL)), _const_spec((D_MODEL, D_MODEL))],
        out_specs=[row(w) for w in widths],
        out_shape=[jax.ShapeDtypeStruct((n, w), dt) for w, dt in zip(widths, dtypes)],
        compiler_params=_params("parallel"), name="gla_proj")(x, g, wq, wk, wv, wr, wa, wu, bu, wma, wmb)


def _fox_prompt_kernel(fq_ref, k_ref, v_ref, cq_ref, ck_ref, o_ref, kt_ref, vt_ref, q_s, k_s, v_s, m_s, l_s, acc_s,
                       *, seq):
    n_full = (seq - N_META) // ATT_BLOCK
    full = n_full * ATT_BLOCK
    lane = lax.broadcasted_iota(jnp.int32, (1, LANES), 1)
    lo = lane < FOX_DH
    qp, cq = fq_ref[0], cq_ref[0]
    q_s[0] = jnp.where(lo, qp, cq)
    q_s[1] = jnp.where(lo, cq, qp)
    kp, ck = k_ref[0].astype(BF16), ck_ref[0]
    k_s[0] = jnp.where(lo, kp, ck)
    k_s[1] = jnp.where(lo, ck, kp)
    v_s[...] = v_ref[0].astype(BF16)

    for src, dst in ((k_ref, kt_ref), (v_ref, vt_ref)):
        dst[0, :, :, :full] = src[0, :full, :].T.reshape(2, FOX_DH, full)
        last_t = src[0, seq - LANES:seq, :].T
        dst[0, :, :, full:seq] = last_t[:, LANES - N_META:].reshape(2, FOX_DH, N_META)

    r_io = lax.broadcasted_iota(jnp.int32, (ATT_BLOCK, ATT_BLOCK), 0)
    c_io = lax.broadcasted_iota(jnp.int32, (ATT_BLOCK, ATT_BLOCK), 1)
    diag_mask = c_io <= r_io
    wide = lambda x: jnp.concatenate([x] * (ATT_BLOCK // LANES), axis=1)

    for j in range(n_full):
        k0 = j * ATT_BLOCK
        keys = slice(k0, k0 + ATT_BLOCK)
        for r0 in range(k0, full, ATT_ROWS):
            rows = slice(r0, min(r0 + ATT_ROWS, full))
            for e in range(2):
                s = _dot_nt(q_s[e, rows, :], k_s[e, keys, :])
                if r0 == k0:
                    top = jnp.where(diag_mask, s[:ATT_BLOCK], NEG)
                    s = top if s.shape[0] == ATT_BLOCK else jnp.concatenate([top, s[ATT_BLOCK:]], axis=0)
                m_cur = jnp.max(s, axis=1, keepdims=True)
                if j == 0:
                    m_next = jnp.broadcast_to(m_cur, (s.shape[0], LANES))
                    p = jnp.exp(s - wide(m_next))
                    l_s[e, rows] = jnp.broadcast_to(jnp.sum(p, axis=1, keepdims=True), m_next.shape)
                    acc_s[e, rows] = _dot(p.astype(BF16), v_s[keys, :])
                else:
                    m_prev = m_s[e, rows]
                    m_next = jnp.maximum(m_prev, m_cur)
                    p = jnp.exp(s - wide(m_next))
                    alpha = jnp.exp(m_prev - m_next)
                    l_s[e, rows] = alpha * l_s[e, rows] + jnp.sum(p, axis=1, keepdims=True)
                    acc_s[e, rows] = alpha * acc_s[e, rows] + _dot(p.astype(BF16), v_s[keys, :])
                m_s[e, rows] = m_next
    o_ref[0, :full, :] = jnp.where(lo, acc_s[0] / l_s[0], acc_s[1] / l_s[1]).astype(o_ref.dtype)

    tr = lax.broadcasted_iota(jnp.int32, (N_META, ATT_BLOCK), 0)
    tc = lax.broadcasted_iota(jnp.int32, (N_META, ATT_BLOCK), 1)
    first_new = ATT_BLOCK - N_META
    tail_mask = (tc >= first_new) & (tc - first_new <= tr)
    last = slice(seq - ATT_BLOCK, seq)
    outs = []
    for e in range(2):
        q = q_s[e, full:seq, :]
        s_old = _dot_nt(q, k_s[e, :full, :])
        s_new = jnp.where(tail_mask, _dot_nt(q, k_s[e, last, :]), NEG)
        m = jnp.maximum(jnp.max(s_old, axis=1, keepdims=True), jnp.max(s_new, axis=1, keepdims=True))
        p_old, p_new = jnp.exp(s_old - m), jnp.exp(s_new - m)
        l = jnp.sum(p_old, axis=1, keepdims=True) + jnp.sum(p_new, axis=1, keepdims=True)
        outs.append((_dot(p_old.astype(BF16), v_s[:full, :]) + _dot(p_new.astype(BF16), v_s[last, :])) / l)
    o_ref[0, full:seq, :] = jnp.where(lo, outs[0], outs[1]).astype(o_ref.dtype)


def _fox_prompt(fq, k, v, cq, ck):
    b, seq, _ = fq.shape
    assert (seq - N_META) % ATT_BLOCK == 0 and seq >= ATT_BLOCK
    full = seq - N_META
    spec = pl.BlockSpec((1, seq, LANES), lambda i, j: (i, 0, j))
    t_spec = pl.BlockSpec((1, 2, FOX_DH, seq), lambda i, j: (i, j, 0, 0))
    t_shape = jax.ShapeDtypeStruct((b, FOX_HEADS, FOX_DH, seq), F32)
    return pl.pallas_call(
        functools.partial(_fox_prompt_kernel, seq=seq), grid=(b, FOX_W // LANES),
        in_specs=[spec] * 5, out_specs=[spec, t_spec, t_spec],
        out_shape=[jax.ShapeDtypeStruct((b, seq, FOX_W), BF16), t_shape, t_shape],
        scratch_shapes=[pltpu.VMEM((2, seq, LANES), BF16), pltpu.VMEM((2, seq, LANES), BF16),
                        pltpu.VMEM((seq, LANES), BF16), pltpu.VMEM((2, full, LANES), F32),
                        pltpu.VMEM((2, full, LANES), F32), pltpu.VMEM((2, full, LANES), F32)],
        compiler_params=_params("parallel", "parallel"), name="fox_prompt")(fq, k, v, cq, ck)


def _gla_prompt_kernel(q_ref, k_ref, g_ref, v_ref, gn_ref, tri_ref, o_ref, st_ref, s_s, *, seq):
    n_blocks = (seq - N_META) // GLA_BLOCK
    scale = GLA_DK ** -0.5
    gn = gn_ref[...]
    s_s[...] = jnp.zeros_like(s_s)

    def block(h, q, k, g, v, rows):
        r_io = lax.broadcasted_iota(jnp.int32, (rows, rows), 0)
        c_io = lax.broadcasted_iota(jnp.int32, (rows, rows), 1)
        intra = (r_io // GLA_CHUNK == c_io // GLA_CHUNK) & (c_io <= r_io)
        col_chunk = lax.broadcasted_iota(jnp.int32, (GLA_DK, rows), 1) // GLA_CHUNK
        tri = tri_ref[:rows, :rows]
        g3 = _split3_cat(g)
        b = _sum3(_dot(tri, g3), GLA_DK)
        b_last = jnp.concatenate(
            [jnp.broadcast_to(b[e - 1:e, :], (GLA_CHUNK, GLA_DK)) for e in range(GLA_CHUNK, rows + 1, GLA_CHUNK)], axis=0)
        kf = k.astype(F32)
        qg = (q.astype(F32) * (scale * jnp.exp(b))).astype(BF16)
        kb = (kf * jnp.exp(-b)).astype(BF16)
        kd_t = (kf * jnp.exp(b_last - b)).T
        decay_t = jnp.exp(b_last).T
        a = jnp.where(intra, _dot_nt(qg, kb), 0.0).astype(BF16)
        o = _dot(a, v)
        outs = []
        for c in range(rows // GLA_CHUNK):
            sl = slice(c * GLA_CHUNK, (c + 1) * GLA_CHUNK)
            s = s_s[h]
            oc = o[sl] + _dot(qg[sl], s.astype(BF16))
            upd = _dot(jnp.where(col_chunk == c, kd_t, 0.0).astype(BF16), v)
            s_s[h] = s * decay_t[:, c * GLA_CHUNK:c * GLA_CHUNK + 1] + upd
            outs.append(_rms(oc, gn))
        return outs

    ks = lambda h: slice(h * GLA_DK, (h + 1) * GLA_DK)
    vs = lambda h: slice(h * GLA_DV, (h + 1) * GLA_DV)

    lead = 2 * GLA_CHUNK - N_META
    pad = lambda ref, cols: jnp.concatenate(
        [jnp.zeros((lead, cols.stop - cols.start), ref.dtype), ref[0, :N_META, cols]], axis=0)
    for h in range(GLA_HEADS):
        outs = block(h, pad(q_ref, ks(h)), pad(k_ref, ks(h)), pad(g_ref, ks(h)), pad(v_ref, vs(h)), 2 * GLA_CHUNK)
        o_ref[0, :N_META, vs(h)] = outs[1][GLA_CHUNK - N_META:].astype(o_ref.dtype)

    def body(n, carry):
        r0 = pl.multiple_of(N_META + n * GLA_BLOCK, N_META)
        rows = pl.ds(r0, GLA_BLOCK)
        for h in range(GLA_HEADS):
            outs = block(h, q_ref[0, rows, ks(h)], k_ref[0, rows, ks(h)], g_ref[0, rows, ks(h)],
                         v_ref[0, rows, vs(h)], GLA_BLOCK)
            o_ref[0, rows, vs(h)] = jnp.concatenate(outs, axis=0).astype(o_ref.dtype)
        return carry

    lax.fori_loop(0, n_blocks, body, 0)
    st_ref[0] = s_s[...]


def _gla_prompt(gq, gk, gl, gv, gn):
    b, seq, _ = gq.shape
    assert (seq - N_META) % GLA_BLOCK == 0
    idx = lax.broadcasted_iota(jnp.int32, (GLA_BLOCK, GLA_BLOCK), 0), lax.broadcasted_iota(jnp.int32, (GLA_BLOCK, GLA_BLOCK), 1)
    same = idx[0] // GLA_CHUNK == idx[1] // GLA_CHUNK
    tri = (same & (idx[1] <= idx[0])).astype(BF16)
    kspec = pl.BlockSpec((1, seq, GLA_KW), lambda i: (i, 0, 0))
    vspec = pl.BlockSpec((1, seq, GLA_VW), lambda i: (i, 0, 0))
    state = (GLA_HEADS, GLA_DK, GLA_DV)
    return pl.pallas_call(
        functools.partial(_gla_prompt_kernel, seq=seq), grid=(b,),
        in_specs=[kspec, kspec, kspec, vspec, _const_spec((1, GLA_DV)),
                  _const_spec((GLA_BLOCK, GLA_BLOCK))],
        out_specs=[vspec, pl.BlockSpec((1,) + state, lambda i: (i, 0, 0, 0))],
        out_shape=[jax.ShapeDtypeStruct((b, seq, GLA_VW), BF16), jax.ShapeDtypeStruct((b,) + state, F32)],
        scratch_shapes=[pltpu.VMEM(state, F32)],
        compiler_params=_params("parallel"), name="gla_prompt")(gq, gk, gl, gv, gn, tri)


def _merge_kernel(x_ref, fo_ref, go_ref, sr_ref, ma_ref, mb_ref, wa_ref, wb_ref, wo_ref, gpost_ref, gpre_ref,
                  x1_ref, h2_ref):
    gated = (go_ref[...].astype(F32) * sr_ref[...].astype(F32)).astype(BF16)
    y = (ma_ref[...].astype(F32) * _dot(fo_ref[...].astype(BF16), wa_ref[...])
         + mb_ref[...].astype(F32) * _dot(gated, wb_ref[...]))
    mix = _dot(y.astype(BF16), wo_ref[...])
    x1 = x_ref[...] + _rms(mix, gpost_ref[...])
    x1_ref[...] = x1
    h2_ref[...] = _rms(x1, gpre_ref[...]).astype(BF16)


def _merge(x, fo, go, sr, ma, mb, wa, wb, wo, gpost, gpre, *, tm):
    n = x.shape[0]
    row = lambda w: pl.BlockSpec((tm, w), lambda i: (i, 0))
    return pl.pallas_call(
        _merge_kernel, grid=(n // tm,),
        in_specs=[row(D_MODEL), row(FOX_W), row(GLA_VW), row(GLA_VW), row(D_MODEL), row(D_MODEL),
                  _const_spec((FOX_W, D_MODEL)), _const_spec((GLA_VW, D_MODEL)), _const_spec((D_MODEL, D_MODEL)),
                  _const_spec((1, D_MODEL)), _const_spec((1, D_MODEL))],
        out_specs=[row(D_MODEL), row(D_MODEL)],
        out_shape=[jax.ShapeDtypeStruct((n, D_MODEL), F32), jax.ShapeDtypeStruct((n, D_MODEL), BF16)],
        compiler_params=_params("parallel"), name="merge")(x, fo, go, sr, ma, mb, wa, wb, wo, gpost, gpre)


def _gelu(x):
    return 0.5 * x * (1.0 + jnp.tanh(0.7978845608028654 * (x + 0.044715 * x * x * x)))


def _ffn_kernel(h_ref, x_ref, wup_ref, cw_ref, cb_ref, wdn_ref, g_ref, *rest, tiles_per_seq):
    if tiles_per_seq:
        y_ref, conv_ref, act_s, u_s, carry_s = rest
    else:
        p2_ref, p1_ref, y_ref, unew_ref, act_s = rest
    h = h_ref[...]
    tm = h.shape[0]
    if tiles_per_seq:
        @pl.when(pl.program_id(0) % tiles_per_seq == 0)
        def _():
            carry_s[...] = jnp.zeros_like(carry_s)

    n_chunks = D_FF // FFN_COLS
    acc, done = None, 0
    for c in range(n_chunks):
        halves = []
        for col in (c * FFN_COLS, D_FF + c * FFN_COLS):
            cols = slice(col, col + FFN_COLS)
            u = _dot(h, wup_ref[:, cols])
            if tiles_per_seq:
                u_s[0:8, :] = carry_s[:, cols]
                u_s[8:8 + tm, :] = u
                prev2, prev1 = u_s[6:6 + tm, :], u_s[7:7 + tm, :]
                carry_s[:, cols] = u_s[tm:tm + 8, :]
                conv_ref[0, :, cols] = u_s[tm + 6:tm + 8, :]
            else:
                prev2, prev1 = p2_ref[:, cols], p1_ref[:, cols]
                unew_ref[:, cols] = u
            halves.append(cb_ref[:, cols] + prev2 * cw_ref[0:1, cols] + prev1 * cw_ref[1:2, cols]
                          + u * cw_ref[2:3, cols])
        val, gate = halves
        act_s[:, c * FFN_COLS:(c + 1) * FFN_COLS] = (_gelu(gate) * val).astype(BF16)
        if (c + 1) % FFN_DOWN_GROUP == 0 or c == n_chunks - 1:
            ready = slice(done * FFN_COLS, (c + 1) * FFN_COLS)
            part = _dot(act_s[:, ready], wdn_ref[ready, :])
            acc = part if acc is None else acc + part
            done = c + 1
    y_ref[...] = x_ref[...] + _rms(acc, g_ref[...])


def _ffn(h, x, wup, cw, cb, wdn, g, *, tm, tiles_per_seq, prev=None):
    n = h.shape[0]
    row = lambda w: pl.BlockSpec((tm, w), lambda i: (i, 0))
    in_specs = [row(D_MODEL), row(D_MODEL), _const_spec((D_MODEL, 2 * D_FF)), _const_spec((CONV_W, 2 * D_FF)),
                _const_spec((1, 2 * D_FF)), _const_spec((D_FF, D_MODEL)), _const_spec((1, D_MODEL))]
    args = [h, x, wup, cw, cb, wdn, g]
    if tiles_per_seq:
        n_seq = n // (tm * tiles_per_seq)
        out_shape = [jax.ShapeDtypeStruct((n, D_MODEL), F32), jax.ShapeDtypeStruct((n_seq, CONV_W - 1, 2 * D_FF), F32)]
        out_specs = [row(D_MODEL), pl.BlockSpec((1, CONV_W - 1, 2 * D_FF), lambda i: (i // tiles_per_seq, 0, 0))]
        scratch = [pltpu.VMEM((tm, D_FF), BF16), pltpu.VMEM((tm + 8, FFN_COLS), F32), pltpu.VMEM((8, 2 * D_FF), F32)]
    else:
        args += list(prev)
        in_specs += [row(2 * D_FF), row(2 * D_FF)]
        out_shape = [jax.ShapeDtypeStruct((n, D_MODEL), F32), jax.ShapeDtypeStruct((n, 2 * D_FF), F32)]
        out_specs = [row(D_MODEL), row(2 * D_FF)]
        scratch = [pltpu.VMEM((tm, D_FF), BF16)]
    return pl.pallas_call(
        functools.partial(_ffn_kernel, tiles_per_seq=tiles_per_seq), grid=(n // tm,),
        in_specs=in_specs, out_specs=out_specs, out_shape=out_shape, scratch_shapes=scratch,
        compiler_params=_params("arbitrary"), name="conv_ffn")(*args)


def _fox_sample_kernel(pt_ref, q_ref, kn_ref, vn_ref, lfn_ref, uo_ref, w_ref, *rest):
    del pt_ref
    npg = PAGES_PER_STEP
    k_refs, v_refs, l_refs = rest[:npg], rest[npg:2 * npg], rest[2 * npg:3 * npg]
    o_ref, qc_s, m_s, l_s, acc_s, carry_s, cn_s, row_s = rest[3 * npg:]
    g = pl.program_id(1)
    eye = lax.broadcasted_iota(jnp.int32, (LANES, LANES), 0) == lax.broadcasted_iota(jnp.int32, (LANES, LANES), 1)
    to_col = lambda r: jnp.sum(jnp.where(eye, r, 0.0), axis=1, keepdims=True)
    to_row = lambda c: jnp.sum(jnp.where(eye, c, 0.0), axis=0, keepdims=True)
    sub = lax.broadcasted_iota(jnp.int32, (FOX_HEADS, LANES), 0)
    lane0 = lax.broadcasted_iota(jnp.int32, (FOX_DH, LANES), 1) == 0

    @pl.when(g == 0)
    def _():
        row_s[...] = jnp.zeros_like(row_s)
        row_s[0:1, 0:FOX_HEADS] = lfn_ref[0]
        cn_s[...] = jnp.broadcast_to(to_col(row_s[0:1, :])[:FOX_HEADS], (FOX_HEADS, LANES))
        carry_s[...] = jnp.zeros_like(carry_s)
        s_new = jnp.zeros((FOX_HEADS, LANES), F32)
        for j in range(FOX_HEADS // 2):
            lanes = slice(j * LANES, (j + 1) * LANES)
            qcol = to_col(q_ref[0, :, lanes])
            qc_s[lanes, :] = jnp.broadcast_to(qcol, (LANES, LANES))
            qk = qcol * to_col(kn_ref[0, :, lanes])
            vcol = to_col(vn_ref[0, :, lanes])
            for e in range(2):
                h = 2 * j + e
                rows = slice(e * FOX_DH, (e + 1) * FOX_DH)
                s_new = jnp.where(sub == h, jnp.sum(qk[rows], axis=0, keepdims=True), s_new)
                acc_s[h] = jnp.where(lane0, vcol[rows], 0.0)
        m_s[...] = s_new
        l_s[...] = jnp.ones_like(l_s)

    lf = jnp.concatenate([l_refs[i][0] for i in range(npg)], axis=0)
    both = _dot(_split3_cat(lf), uo_ref[...])
    in_page, page_tot = both[:, :LANES], both[:, LANES:]
    later = _sum3(_dot(w_ref[...], _split3_cat(page_tot)), LANES)
    bias = in_page + later + jnp.concatenate([carry_s[...] + cn_s[...]] * npg, axis=0)
    carry_s[...] = carry_s[...] + later[:FOX_HEADS] + page_tot[:FOX_HEADS]

    pages = [jnp.zeros((FOX_HEADS, LANES), F32)] * npg
    for h in range(FOX_HEADS):
        qc = qc_s[h * FOX_DH:(h + 1) * FOX_DH, :]
        for i in range(npg):
            pages[i] = jnp.where(sub == h, jnp.sum(qc * k_refs[i][0, h], axis=0, keepdims=True), pages[i])
    s = jnp.concatenate(pages, axis=0) + bias
    per_head = lambda x: x.reshape(npg, FOX_HEADS, LANES)
    m_prev = m_s[...]
    m_next = jnp.maximum(m_prev, jnp.max(jnp.max(per_head(s), axis=0), axis=1, keepdims=True))
    p = jnp.exp(s - jnp.concatenate([m_next] * npg, axis=0))
    alpha = jnp.exp(m_prev - m_next)
    l_s[...] = alpha * l_s[...] + jnp.sum(jnp.sum(per_head(p), axis=0), axis=1, keepdims=True)
    m_s[...] = m_next
    for h in range(FOX_HEADS):
        acc = acc_s[h] * alpha[h:h + 1, :]
        for i in range(npg):
            r = i * FOX_HEADS + h
            acc = acc + p[r:r + 1, :] * v_refs[i][0, h]
        acc_s[h] = acc

    @pl.when(g == pl.num_programs(1) - 1)
    def _():
        for j in range(FOX_HEADS // 2):
            cols = [jnp.sum(acc_s[2 * j + e], axis=1, keepdims=True) / l_s[2 * j + e:2 * j + e + 1, 0:1]
                    for e in range(2)]
            o_ref[0, :, j * LANES:(j + 1) * LANES] = to_row(jnp.concatenate(cols, axis=0))


def _fox_sample(q, kn, vn, lfn, cache_kt, cache_vt, cache_lt, page_table):
    db = q.shape[0]
    n_pages = page_table.shape[1]
    npg = PAGES_PER_STEP
    assert n_pages % npg == 0
    groups = n_pages // npg
    tok = jnp.arange(PAGE_SIZE)
    after = (tok[:, None] > tok[None, :]).astype(BF16)
    uo = jnp.concatenate([jnp.concatenate([after, jnp.ones_like(after)], axis=1)] * 3, axis=0)
    r = jnp.arange(npg * FOX_HEADS)
    w = ((r[:, None] % FOX_HEADS == r[None, :] % FOX_HEADS) & (r[None, :] // FOX_HEADS > r[:, None] // FOX_HEADS)).astype(BF16)

    def page_spec(i, shape):
        return pl.BlockSpec((1,) + shape, lambda b, g, pt: (pt[b, (groups - 1 - g) * npg + i],) + (0,) * len(shape))

    vec = lambda width: pl.BlockSpec((1, 1, width), lambda b, g, pt: (b, 0, 0))
    const = lambda shape: pl.BlockSpec(shape, lambda b, g, pt: (0,) * len(shape))
    kv_page = (FOX_HEADS, FOX_DH, PAGE_SIZE)
    grid_spec = pltpu.PrefetchScalarGridSpec(
        num_scalar_prefetch=1, grid=(db, groups),
        in_specs=[vec(FOX_W), vec(FOX_W), vec(FOX_W), vec(FOX_HEADS), const(uo.shape), const(w.shape)]
        + [page_spec(i, kv_page) for i in range(npg)] * 2 + [page_spec(i, (FOX_HEADS, PAGE_SIZE)) for i in range(npg)],
        out_specs=vec(FOX_W),
        scratch_shapes=[pltpu.VMEM((FOX_W, LANES), F32), pltpu.VMEM((FOX_HEADS, LANES), F32),
                        pltpu.VMEM((FOX_HEADS, LANES), F32), pltpu.VMEM((FOX_HEADS, FOX_DH, LANES), F32),
                        pltpu.VMEM((FOX_HEADS, LANES), F32), pltpu.VMEM((FOX_HEADS, LANES), F32),
                        pltpu.VMEM((8, LANES), F32)])
    return pl.pallas_call(
        _fox_sample_kernel, grid_spec=grid_spec, out_shape=jax.ShapeDtypeStruct((db, 1, FOX_W), F32),
        compiler_params=_params("parallel", "arbitrary"), name="fox_sample")(
            page_table, q, kn, vn, lfn, uo, w, *([cache_kt] * npg), *([cache_vt] * npg), *([cache_lt] * npg))


def _gla_sample_kernel(q_ref, k_ref, g_ref, v_ref, s_ref, gn_ref, o_ref, sn_ref):
    eye = lax.broadcasted_iota(jnp.int32, (GLA_DK, GLA_DK), 0) == lax.broadcasted_iota(jnp.int32, (GLA_DK, GLA_DK), 1)
    col = lambda r: jnp.sum(jnp.where(eye, r, 0.0), axis=1, keepdims=True)
    scale = GLA_DK ** -0.5
    for i in range(q_ref.shape[0]):
        for h in range(GLA_HEADS):
            ks = slice(h * GLA_DK, (h + 1) * GLA_DK)
            vs = slice(h * GLA_DV, (h + 1) * GLA_DV)
            s = s_ref[i, h] * col(jnp.exp(g_ref[i, :, ks])) + col(k_ref[i, :, ks]) * v_ref[i, :, vs]
            sn_ref[i, h] = s
            qc = col(q_ref[i, :, ks] * scale).astype(BF16).astype(F32)
            o = jnp.sum(qc * s.astype(BF16).astype(F32), axis=0, keepdims=True)
            o_ref[i, :, vs] = _rms(o, gn_ref[...])


def _gla_sample(gq, gk, gl, gv, state, gn):
    db = gq.shape[0]
    sb = GLA_SAMPLES_PER_STEP
    assert db % sb == 0
    kspec = pl.BlockSpec((sb, 1, GLA_KW), lambda b: (b, 0, 0))
    vspec = pl.BlockSpec((sb, 1, GLA_VW), lambda b: (b, 0, 0))
    sspec = pl.BlockSpec((sb, GLA_HEADS, GLA_DK, GLA_DV), lambda b: (b, 0, 0, 0))
    return pl.pallas_call(
        _gla_sample_kernel, grid=(db // sb,),
        in_specs=[kspec, kspec, kspec, vspec, sspec, pl.BlockSpec((1, GLA_DV), lambda b: (0, 0))],
        out_specs=[vspec, sspec],
        out_shape=[jax.ShapeDtypeStruct((db, 1, GLA_VW), F32), jax.ShapeDtypeStruct(state.shape, F32)],
        compiler_params=_params("parallel"), name="gla_sample")(gq, gk, gl, gv, state, gn)


def _layer_weights(l, w_in, b_forget, w_gla_gate_up, b_gla_gate):
    w = w_in[l]
    o = 0
    parts = {}
    for name, width in (("fq", FOX_W), ("fk", FOX_W), ("fv", FOX_W), ("ff", FOX_HEADS), ("gq", GLA_KW), ("gk", GLA_KW),
                        ("gv", GLA_VW), ("gr", GLA_VW), ("ga", GLA_RANK), ("ma", D_MODEL), ("mb", D_MODEL)):
        parts[name] = w[:, o:o + width]
        o += width
    bf = lambda a: a.astype(BF16)
    pad_cols = lambda a: jnp.pad(a, ((0, 0), (0, LANES - a.shape[1])))
    fox = (bf(parts["fq"] * FOX_DH ** -0.5), bf(parts["fk"]), bf(parts["fv"]), bf(pad_cols(parts["ff"])),
           pad_cols(b_forget[l][None, :].astype(F32)))
    wu = jnp.pad(w_gla_gate_up[l], ((0, LANES - GLA_RANK), (0, 0)))
    gla = (bf(parts["gq"]), bf(parts["gk"]), bf(parts["gv"]), bf(parts["gr"]), bf(pad_cols(parts["ga"])), bf(wu),
           b_gla_gate[l][None, :].astype(F32), bf(parts["ma"]), bf(parts["mb"]))
    return fox, gla


def kernel(x_prompt, x_sample, cache_k, cache_v, cache_logf, state_gla, state_conv, page_table, meta_tokens, g_pre_mix, w_in, b_forget, w_gla_gate_up, b_gla_gate, gla_norm_g, w_fox_up, w_gla_up, w_out, g_post_mix, g_pre_ffn, w_up, conv_w, conv_b, w_down, g_post_ffn):
    bsz, seq0, _ = x_prompt.shape
    db, ds, _ = x_sample.shape
    depth = w_in.shape[0]
    assert depth == 1 and ds == 1
    seq = N_META + seq0
    tiles_per_seq = 3
    tm_seq = seq // tiles_per_seq
    assert tm_seq * tiles_per_seq == seq and tm_seq % 16 == 0
    n_p = bsz * seq
    tm_row = tm_seq
    n_pool = cache_k.shape[1]
    l = 0
    row1 = lambda a: a[l][None, :].astype(F32)

    fox_w, gla_w = _layer_weights(l, w_in, b_forget, w_gla_gate_up, b_gla_gate)
    g_pre, gn = row1(g_pre_mix), row1(gla_norm_g)
    wa, wb, wo = w_fox_up[l].astype(BF16), w_gla_up[l].astype(BF16), w_out[l].astype(BF16)
    g_post, g_ffn_pre, g_ffn_post = row1(g_post_mix), row1(g_pre_ffn), row1(g_post_ffn)
    wup, wdn = w_up[l].astype(BF16), w_down[l].astype(BF16)
    cw, cb = conv_w[l].astype(F32), row1(conv_b)

    meta = jnp.broadcast_to(meta_tokens.astype(x_prompt.dtype), (bsz, N_META, D_MODEL))
    xp = jnp.concatenate([meta, x_prompt], axis=1).reshape(n_p, D_MODEL)
    fq, fk, fv, lf, cq, ck = _fox_proj(xp, g_pre, *fox_w, tm=tm_seq, tiles_per_seq=tiles_per_seq, q_dtype=BF16)
    gq, gk, gv, gl, sr, ma, mb = _gla_proj(xp, g_pre, *gla_w, tm=tm_row, mid_dtype=BF16)
    s3 = lambda a: a.reshape(bsz, seq, a.shape[-1])
    fo, kt, vt = _fox_prompt(s3(fq), s3(fk), s3(fv), s3(cq), s3(ck))
    token_major = lambda a: jnp.transpose(a, (0, 3, 1, 2))[None]
    go, gla_p = _gla_prompt(s3(gq), s3(gk), s3(gl), s3(gv), gn)
    x1, h2 = _merge(xp, fo.reshape(n_p, FOX_W), go.reshape(n_p, GLA_VW), sr, ma, mb, wa, wb, wo, g_post, g_ffn_pre,
                    tm=tm_row)
    yp, conv_p = _ffn(h2, x1, wup, cw, cb, wdn, g_ffn_post, tm=tm_seq, tiles_per_seq=tiles_per_seq)
    y_prompt = yp.reshape(bsz, seq, D_MODEL)[:, N_META:]

    xs = x_sample.reshape(db, D_MODEL)
    sq, sk, sv, slf = _fox_proj(xs, g_pre, *fox_w, tm=db, tiles_per_seq=0, q_dtype=F32)
    tq, tk, tv, tl, tsr, tma, tmb = _gla_proj(xs, g_pre, *gla_w, tm=db, mid_dtype=F32)
    v3 = lambda a: a.reshape(db, 1, a.shape[-1])
    so = _fox_sample(v3(sq), v3(sk), v3(sv), v3(slf),
                     jnp.transpose(cache_k[l], (0, 2, 3, 1)), jnp.transpose(cache_v[l], (0, 2, 3, 1)),
                     jnp.transpose(cache_logf[l], (0, 2, 1)), page_table)
    sgo, gla_s = _gla_sample(v3(tq), v3(tk), v3(tl), v3(tv), state_gla[l], gn)
    sx1, sh2 = _merge(xs, so.reshape(db, FOX_W), sgo.reshape(db, GLA_VW), tsr, tma, tmb, wa, wb, wo, g_post,
                      g_ffn_pre, tm=db)
    ys, u_new = _ffn(sh2, sx1, wup, cw, cb, wdn, g_ffn_post, tm=db, tiles_per_seq=0,
                     prev=(state_conv[l][:, 0, :], state_conv[l][:, 1, :]))
    conv_s = jnp.stack([state_conv[l][:, 1, :], u_new], axis=1)

    heads = lambda a, n: a.reshape(1, n, -1, FOX_HEADS, FOX_DH)
    return (y_prompt, ys.reshape(db, 1, D_MODEL),
            token_major(kt), token_major(vt), lf.reshape(1, bsz, seq, FOX_HEADS), gla_p[None], conv_p[None],
            heads(sk, db), heads(sv, db), slf.reshape(1, db, 1, FOX_HEADS), gla_s[None], conv_s[None])
```
